```python
import math
import jax, jax.numpy as jnp
from jax import lax
import numpy as np

D_MODEL = 1024
BATCH = 8
SEQ = 4096
DEPTH = 1
DEC_BATCH = 4
DEC_SEQ = 4096
PAST_LEN = 128

GRID_W = 64
P_DIM = 256
GLA_HEADS = 4
GLA_DK = 64
GLA_DV = 128
GLA_GATE_RANK = 16
GLA_TAU = 16.0
GLA_CHUNK = 64
ATTN_HEADS = 8
ATTN_KV_HEADS = 2
ATTN_DH = 64
Q_BLOCK = 128
ROPE_THETA = 10000.0
D_FF = 2816
LN_EPS = 1e-5
QK_EPS = 1e-6
GN_EPS = 1e-5
DEEPNORM_ALPHA = (2.0 * DEPTH) ** 0.25
DEEPNORM_BETA = (8.0 * DEPTH) ** -0.25

GLA_QK_W = GLA_HEADS * GLA_DK
GLA_V_W = GLA_HEADS * GLA_DV
ATTN_Q_W = ATTN_HEADS * ATTN_DH
ATTN_KV_W = ATTN_KV_HEADS * ATTN_DH
IN_SPLITS = (GLA_QK_W, GLA_QK_W, GLA_V_W, GLA_V_W, GLA_GATE_RANK, GLA_GATE_RANK, ATTN_Q_W, ATTN_KV_W, ATTN_KV_W)
D_IN = sum(IN_SPLITS)
MIX_W = GLA_V_W + ATTN_Q_W

kernel_name = "hymba_gla_axialrope_gqa_macaron_deepnorm_encoder"


def layer_norm(x, g, b):
    xf = x.astype(jnp.float32)
    mu = jnp.mean(xf, axis=-1, keepdims=True)
    var = jnp.mean(jnp.square(xf - mu), axis=-1, keepdims=True)
    y = (xf - mu) * lax.rsqrt(var + LN_EPS)
    return (y * g.astype(jnp.float32) + b.astype(jnp.float32)).astype(x.dtype)


def rms_norm(x, g, eps):
    xf = x.astype(jnp.float32)
    y = xf * lax.rsqrt(jnp.mean(jnp.square(xf), axis=-1, keepdims=True) + eps)
    return (y * g.astype(jnp.float32)).astype(x.dtype)


def swiglu(x, w_g, w_u, w_d):
    return (jax.nn.silu(x @ w_g) * (x @ w_u)) @ w_d


def gla_direction(q, k, v, log_a, inclusive):
    bsz, n, h, dk = q.shape
    dv = v.shape[-1]
    c = GLA_CHUNK
    nc = n // c
    q = q.reshape(bsz, nc, c, h, dk)
    k = k.reshape(bsz, nc, c, h, dk)
    v = v.reshape(bsz, nc, c, h, dv)
    cum = jnp.cumsum(log_a.reshape(bsz, nc, c, h, dk), axis=2)
    mid = cum[:, :, c // 2 - 1:c // 2]
    last = cum[:, :, c - 1:]
    scores = jnp.einsum('bnihd,bnjhd->bnhij', q * jnp.exp(cum - mid), k * jnp.exp(mid - cum))
    mask = jnp.tril(jnp.ones((c, c), dtype=bool), 0 if inclusive else -1)
    scores = jnp.where(mask, scores, 0.0)
    o_intra = jnp.einsum('bnhij,bnjhe->bnihe', scores, v)
    upd = jnp.einsum('bnjhd,bnjhe->bnhde', k * jnp.exp(last - cum), v)
    chunk_decay = jnp.exp(last[:, :, 0])

    def step(state, inp):
        dec, u = inp
        return dec[..., None] * state + u, state

    s0 = jnp.zeros((bsz, h, dk, dv), q.dtype)
    _, states = lax.scan(step, s0, (jnp.moveaxis(chunk_decay, 1, 0), jnp.moveaxis(upd, 1, 0)))
    states = jnp.moveaxis(states, 0, 1)
    o_inter = jnp.einsum('bnihd,bnhde->bnihe', q * jnp.exp(cum), states)
    return (o_intra + o_inter).reshape(bsz, n, h, dv)


def gla_mixer(gq, gk, gv, gr, zf, zb, w2f, b2f, w2b, b2b, gn_g):
    bsz, n, _ = gq.shape
    dt = gq.dtype
    f32 = jnp.float32
    q = gq.astype(f32).reshape(bsz, n, GLA_HEADS, GLA_DK) * (GLA_DK ** -0.5)
    k = gk.astype(f32).reshape(bsz, n, GLA_HEADS, GLA_DK)
    v = gv.astype(f32).reshape(bsz, n, GLA_HEADS, GLA_DV)
    log_af = (jax.nn.log_sigmoid((zf @ w2f + b2f).astype(f32)) / GLA_TAU).reshape(bsz, n, GLA_HEADS, GLA_DK)
    log_ab = (jax.nn.log_sigmoid((zb @ w2b + b2b).astype(f32)) / GLA_TAU).reshape(bsz, n, GLA_HEADS, GLA_DK)
    o_f = gla_direction(q, k, v, log_af, True)
    o_b = gla_direction(q[:, ::-1], k[:, ::-1], v[:, ::-1], log_ab[:, ::-1], False)[:, ::-1]
    o = rms_norm(o_f + o_b, gn_g, GN_EPS).reshape(bsz, n, GLA_V_W)
    return (o * jax.nn.silu(gr.astype(f32))).astype(dt)


def axial_rope_tables(n):
    rows = n // GRID_W
    row = jnp.repeat(jnp.arange(rows, dtype=jnp.float32), GRID_W)
    col = jnp.tile(jnp.arange(GRID_W, dtype=jnp.float32), rows)
    axis_dim = ATTN_DH // 2
    inv_freq = ROPE_THETA ** (-jnp.arange(0, axis_dim, 2, dtype=jnp.float32) / axis_dim)
    ang_r = (row[:, None] * inv_freq)[:, None, :]
    ang_c = (col[:, None] * inv_freq)[:, None, :]
    return jnp.cos(ang_r), jnp.sin(ang_r), jnp.cos(ang_c), jnp.sin(ang_c)


def rotate(x, cos, sin):
    half = x.shape[-1] // 2
    x1, x2 = x[..., :half], x[..., half:]
    return jnp.concatenate([x1 * cos - x2 * sin, x2 * cos + x1 * sin], axis=-1)


def apply_axial_rope(x, tabs):
    cos_r, sin_r, cos_c, sin_c = tabs
    axis_dim = ATTN_DH // 2
    xr = rotate(x[..., :axis_dim].astype(jnp.float32), cos_r, sin_r)
    xc = rotate(x[..., axis_dim:].astype(jnp.float32), cos_c, sin_c)
    return jnp.concatenate([xr, xc], axis=-1).astype(x.dtype)


def gqa_mixer(aq, ak, av, qn_g, kn_g):
    bsz, n, _ = aq.shape
    grp = ATTN_HEADS // ATTN_KV_HEADS
    q = rms_norm(aq.reshape(bsz, n, ATTN_HEADS, ATTN_DH), qn_g, QK_EPS)
    k = rms_norm(ak.reshape(bsz, n, ATTN_KV_HEADS, ATTN_DH), kn_g, QK_EPS)
    v = av.reshape(bsz, n, ATTN_KV_HEADS, ATTN_DH)
    tabs = axial_rope_tables(n)
    q = apply_axial_rope(q, tabs)
    k = apply_axial_rope(k, tabs)
    q = q.reshape(bsz, n // Q_BLOCK, Q_BLOCK, ATTN_KV_HEADS, grp, ATTN_DH)
    q = jnp.moveaxis(q, 1, 0)
    scale = ATTN_DH ** -0.5

    def block(qb):
        s = jnp.einsum('bqkgd,bskd->bkgqs', qb, k, preferred_element_type=jnp.float32) * scale
        p = jax.nn.softmax(s, axis=-1)
        return jnp.einsum('bkgqs,bskd->bqkgd', p.astype(v.dtype), v)

    o = lax.map(block, q)
    return jnp.moveaxis(o, 0, 1).reshape(bsz, n, ATTN_Q_W)


def encoder_layer(x, p, ffn1_wg, ffn1_wu, ffn1_wd, ln1_g, ln1_b, w_in,
                  gla_w2f, gla_b2f, gla_w2b, gla_b2b, gla_gn_g, q_norm_g, k_norm_g,
                  w_out, ln2_g, ln2_b, ffn2_wg, ffn2_wu, ffn2_wd, ln3_g, ln3_b,
                  w_pg, b_pg, w_pe):
    a = DEEPNORM_ALPHA
    h = layer_norm(a * x + 0.5 * swiglu(x, ffn1_wg, ffn1_wu, ffn1_wd), ln1_g, ln1_b)
    proj = h @ w_in
    offs = [int(o) for o in np.cumsum(IN_SPLITS)[:-1]]
    gq, gk, gv, gr, zf, zb, aq, ak, av = jnp.split(proj, offs, axis=-1)
    o_gla = gla_mixer(gq, gk, gv, gr, zf, zb, gla_w2f, gla_b2f, gla_w2b, gla_b2b, gla_gn_g)
    o_att = gqa_mixer(aq, ak, av, q_norm_g, k_norm_g)
    mix = jnp.concatenate([o_gla, o_att], axis=-1) @ w_out
    h = layer_norm(a * h + mix, ln2_g, ln2_b)
    h = layer_norm(a * h + 0.5 * swiglu(h, ffn2_wg, ffn2_wu, ffn2_wd), ln3_g, ln3_b)
    return h + jax.nn.sigmoid(h @ w_pg + b_pg) * (p @ w_pe)


def setup_inputs(seed: int = 0) -> dict:
    key = jax.random.key(seed)
    ks = jax.random.split(key, 40)
    f32 = jnp.float32
    nrm = lambda k, shape, s: jax.random.normal(k, shape, f32) * s
    gain = lambda k, shape: 1.0 + 0.02 * jax.random.normal(k, shape, f32)
    beta = DEEPNORM_BETA
    L = DEPTH
    return {
        "x_prompt": nrm(ks[0], (BATCH, SEQ, D_MODEL), 1.0),
        "x_sample": nrm(ks[1], (DEC_BATCH, DEC_SEQ, D_MODEL), 1.0),
        "p_prompt": nrm(ks[2], (DEPTH, BATCH, SEQ, P_DIM), 1.0),
        "p_sample": nrm(ks[3], (DEPTH, DEC_BATCH, DEC_SEQ, P_DIM), 1.0),
        "ffn1_wg": nrm(ks[4], (L, D_MODEL, D_FF), D_MODEL ** -0.5),
        "ffn1_wu": nrm(ks[5], (L, D_MODEL, D_FF), D_MODEL ** -0.5),
        "ffn1_wd": nrm(ks[6], (L, D_FF, D_MODEL), beta * D_FF ** -0.5),
        "ln1_g": gain(ks[7], (L, D_MODEL)),
        "ln1_b": nrm(ks[8], (L, D_MODEL), 0.02),
        "w_in": nrm(ks[9], (L, D_MODEL, D_IN), D_MODEL ** -0.5),
        "gla_w2f": nrm(ks[10], (L, GLA_GATE_RANK, GLA_QK_W), GLA_GATE_RANK ** -0.5),
        "gla_b2f": nrm(ks[11], (L, GLA_QK_W), 0.1),
        "gla_w2b": nrm(ks[12], (L, GLA_GATE_RANK, GLA_QK_W), GLA_GATE_RANK ** -0.5),
        "gla_b2b": nrm(ks[13], (L, GLA_QK_W), 0.1),
        "gla_gn_g": gain(ks[14], (L, GLA_DV)),
        "q_norm_g": gain(ks[15], (L, ATTN_DH)),
        "k_norm_g": gain(ks[16], (L, ATTN_DH)),
        "w_out": nrm(ks[17], (L, MIX_W, D_MODEL), beta * MIX_W ** -0.5),
        "ln2_g": gain(ks[18], (L, D_MODEL)),
        "ln2_b": nrm(ks[19], (L, D_MODEL), 0.02),
        "ffn2_wg": nrm(ks[20], (L, D_MODEL, D_FF), D_MODEL ** -0.5),
        "ffn2_wu": nrm(ks[21], (L, D_MODEL, D_FF), D_MODEL ** -0.5),
        "ffn2_wd": nrm(ks[22], (L, D_FF, D_MODEL), beta * D_FF ** -0.5),
        "ln3_g": gain(ks[23], (L, D_MODEL)),
        "ln3_b": nrm(ks[24], (L, D_MODEL), 0.02),
        "w_pg": nrm(ks[25], (L, D_MODEL, D_MODEL), D_MODEL ** -0.5),
        "b_pg": nrm(ks[26], (L, D_MODEL), 0.02),
        "w_pe": nrm(ks[27], (L, P_DIM, D_MODEL), beta * P_DIM ** -0.5),
    }


def reference(x_prompt, x_sample, p_prompt, p_sample, ffn1_wg, ffn1_wu, ffn1_wd, ln1_g, ln1_b,
              w_in, gla_w2f, gla_b2f, gla_w2b, gla_b2b, gla_gn_g, q_norm_g, k_norm_g, w_out,
              ln2_g, ln2_b, ffn2_wg, ffn2_wu, ffn2_wd, ln3_g, ln3_b, w_pg, b_pg, w_pe):
    y_prompt = x_prompt
    y_sample = x_sample
    for i in range(DEPTH):
        layer_w = (ffn1_wg[i], ffn1_wu[i], ffn1_wd[i], ln1_g[i], ln1_b[i], w_in[i],
                   gla_w2f[i], gla_b2f[i], gla_w2b[i], gla_b2b[i], gla_gn_g[i],
                   q_norm_g[i], k_norm_g[i], w_out[i], ln2_g[i], ln2_b[i],
                   ffn2_wg[i], ffn2_wu[i], ffn2_wd[i], ln3_g[i], ln3_b[i],
                   w_pg[i], b_pg[i], w_pe[i])
        y_prompt = encoder_layer(y_prompt, p_prompt[i], *layer_w)
        y_sample = encoder_layer(y_sample, p_sample[i], *layer_w)
    return (y_prompt, y_sample)
```

```python
import functools

import jax
import jax.numpy as jnp
from jax import lax
from jax.experimental import pallas as pl
from jax.experimental.pallas import tpu as pltpu

F32 = jnp.float32
BF16 = jnp.bfloat16

D_MODEL = 1024
D_FF = 2816
P_DIM = 256
GRID_W = 64
DEPTH = 1
GLA_HEADS = 4
GLA_DK = 64
GLA_DV = 128
GLA_GATE_RANK = 16
GLA_TAU = 16.0
GLA_CHUNK = 64
ATTN_HEADS = 8
ATTN_KV_HEADS = 2
ATTN_DH = 64
ROPE_THETA = 10000.0
LN_EPS = 1e-5
QK_EPS = 1e-6
GN_EPS = 1e-5
DEEPNORM_ALPHA = (2.0 * DEPTH) ** 0.25

GLA_QK_W = GLA_HEADS * GLA_DK
GLA_V_W = GLA_HEADS * GLA_DV
ATTN_Q_W = ATTN_HEADS * ATTN_DH
ATTN_KV_W = ATTN_KV_HEADS * ATTN_DH
Z_W = 2 * GLA_GATE_RANK
OFF_GQK = 0
OFF_GV = OFF_GQK + 2 * GLA_QK_W
OFF_GR = OFF_GV + GLA_V_W
OFF_AQ = OFF_GR + GLA_V_W
OFF_AK = OFF_AQ + ATTN_Q_W
OFF_AV = OFF_AK + ATTN_KV_W
OFF_Z = OFF_AV + ATTN_KV_W
D_IN = OFF_Z + Z_W

LANES = 128
ROPE_HALF = ATTN_DH // 4
QK_SCALE = ATTN_DH ** -0.5
GLA_SCALE = GLA_DK ** -0.5

TOKEN_TILE = 256
GLA_TILE = 512
CUM_TILE = 256
ATTN_Q_TILE = 256
VMEM_LIMIT = 56 * 1024 * 1024


def _const_spec(shape):
    return pl.BlockSpec(shape, lambda *_: (0,) * len(shape), pipeline_mode=pl.Buffered(1))


def _dot(a, b):
    return jnp.dot(a, b, preferred_element_type=F32)


def _dot_nt(a, b):
    return lax.dot_general(a, b, (((1,), (1,)), ((), ())), preferred_element_type=F32)


def _split_bf16(x):
    hi = x.astype(BF16)
    lo = (x - hi.astype(F32)).astype(BF16)
    return hi, lo


def _layer_norm(y, g, b):
    mu = jnp.mean(y, axis=-1, keepdims=True)
    d = y - mu
    var = jnp.mean(d * d, axis=-1, keepdims=True)
    return d * lax.rsqrt(var + LN_EPS) * g + b


def _swiglu(xb, wg_ref, wu_ref, wd_ref):
    g = _dot(xb, wg_ref[...])
    u = _dot(xb, wu_ref[...])
    hid = (g * jax.nn.sigmoid(g)) * u
    return _dot(hid.astype(BF16), wd_ref[...])


def _head_rms(x, avg, gain, eps):
    hi, lo = _split_bf16(x * x)
    ms = _dot(hi, avg) + _dot(lo, avg)
    return x * lax.rsqrt(ms + eps) * gain


def _rope(xs, c, sa, sb):
    return (xs * c + pltpu.roll(xs, LANES - ROPE_HALF, 1) * sa
            + pltpu.roll(xs, ROPE_HALF, 1) * sb)


def _kernel_a(x_ref, wg_ref, wu_ref, wd_ref, lng_ref, lnb_ref, win_ref, avg_ref, qg_ref, kg_ref,
              c_ref, sa_ref, sb_ref,
              h_ref, gqk_ref, gv_ref, gr_ref, z_ref, q_ref, k_ref, v_ref):
    x = x_ref[0]
    f = _swiglu(x.astype(BF16), wg_ref, wu_ref, wd_ref)
    h = _layer_norm(DEEPNORM_ALPHA * x + 0.5 * f, lng_ref[...], lnb_ref[...])
    h_ref[0] = h
    proj = _dot(h.astype(BF16), win_ref[...])
    gqk_ref[0] = proj[:, OFF_GQK:OFF_GV]
    gv_ref[0] = proj[:, OFF_GV:OFF_GR].astype(BF16)
    gr_ref[0] = proj[:, OFF_GR:OFF_AQ]
    z_ref[0] = proj[:, OFF_Z:D_IN]
    c, sa, sb = c_ref[...], sa_ref[...], sb_ref[...]
    avg = avg_ref[...]
    qn = _head_rms(proj[:, OFF_AQ:OFF_AK], avg, qg_ref[...], QK_EPS)
    for j in range(ATTN_Q_W // LANES):
        sl = slice(LANES * j, LANES * (j + 1))
        q_ref[0, :, sl] = (_rope(qn[:, sl], c, sa, sb) * QK_SCALE).astype(BF16)
    kn = _head_rms(proj[:, OFF_AK:OFF_AV], avg[:ATTN_KV_W, :ATTN_KV_W], kg_ref[...], QK_EPS)
    kr = _rope(kn, c, sa, sb).astype(BF16)
    vb = proj[:, OFF_AV:OFF_Z].astype(BF16)
    for g in range(ATTN_KV_HEADS):
        k_ref[0, g] = kr[:, ATTN_DH * g:ATTN_DH * (g + 1)]
        v_ref[0, g] = vb[:, ATTN_DH * g:ATTN_DH * (g + 1)]


def _call_a(x, w, tabs):
    bsz, n, _ = x.shape
    tm = TOKEN_TILE
    tok = lambda width: pl.BlockSpec((1, tm, width), lambda b, i: (b, i, 0))
    tab = pl.BlockSpec((tm, LANES), lambda b, i: (i, 0))
    kv_spec = pl.BlockSpec((1, ATTN_KV_HEADS, tm, ATTN_DH), lambda b, i: (b, 0, i, 0))
    out_shape = (
        jax.ShapeDtypeStruct((bsz, n, D_MODEL), F32),
        jax.ShapeDtypeStruct((bsz, n, 2 * GLA_QK_W), F32),
        jax.ShapeDtypeStruct((bsz, n, GLA_V_W), BF16),
        jax.ShapeDtypeStruct((bsz, n, GLA_V_W), F32),
        jax.ShapeDtypeStruct((bsz, n, Z_W), F32),
        jax.ShapeDtypeStruct((bsz, n, ATTN_Q_W), BF16),
        jax.ShapeDtypeStruct((bsz, ATTN_KV_HEADS, n, ATTN_DH), BF16),
        jax.ShapeDtypeStruct((bsz, ATTN_KV_HEADS, n, ATTN_DH), BF16),
    )
    return pl.pallas_call(
        _kernel_a,
        grid=(bsz, n // tm),
        in_specs=[tok(D_MODEL),
                  _const_spec((D_MODEL, D_FF)), _const_spec((D_MODEL, D_FF)), _const_spec((D_FF, D_MODEL)),
                  _const_spec((1, D_MODEL)), _const_spec((1, D_MODEL)),
                  _const_spec((D_MODEL, D_IN)), _const_spec((ATTN_Q_W, ATTN_Q_W)),
                  _const_spec((1, ATTN_Q_W)), _const_spec((1, ATTN_KV_W)),
                  tab, tab, tab],
        out_specs=(tok(D_MODEL), tok(2 * GLA_QK_W), tok(GLA_V_W), tok(GLA_V_W), tok(Z_W), tok(ATTN_Q_W),
                   kv_spec, kv_spec),
        out_shape=out_shape,
        compiler_params=pltpu.CompilerParams(dimension_semantics=("arbitrary", "arbitrary"),
                                             vmem_limit_bytes=VMEM_LIMIT),
        name="ffn1_inproj",
    )(x, w["ffn1_wg"], w["ffn1_wu"], w["ffn1_wd"], w["ln1_g"], w["ln1_b"], w["w_in"], w["avg"],
      w["q_gain"], w["k_gain"], *tabs)


def _gla_block(gqk_ref, gv_ref, z_ref, w2_ref, b2_ref, tri_ref, out_ref, state_ref, d, reverse):
    tn = gqk_ref.shape[1]
    c_len = GLA_CHUNK
    pre = _dot(z_ref[0].astype(BF16), w2_ref[...]) + b2_ref[...]
    log_a = (jnp.minimum(pre, 0.0) - jnp.log1p(jnp.exp(-jnp.abs(pre)))) * (1.0 / GLA_TAU)
    hi, lo = _split_bf16(log_a)
    tri = tri_ref[...]
    cum = jnp.concatenate(
        [_dot(tri, hi[r:r + CUM_TILE]) + _dot(tri, lo[r:r + CUM_TILE]) for r in range(0, tn, CUM_TILE)],
        axis=0)

    lane_head = lax.broadcasted_iota(jnp.int32, (c_len, GLA_QK_W), 1) // GLA_DK
    head_mask = [(lane_head == h).astype(F32) for h in range(GLA_HEADS)]
    row = lax.broadcasted_iota(jnp.int32, (GLA_HEADS * c_len, c_len), 0) % c_len
    col = lax.broadcasted_iota(jnp.int32, (GLA_HEADS * c_len, c_len), 1)
    keep = (col > row) if reverse else (col <= row)

    n_chunks = tn // c_len
    for c in (range(n_chunks - 1, -1, -1) if reverse else range(n_chunks)):
        r0 = c_len * c
        cc = cum[r0:r0 + c_len]
        mid = cc[c_len // 2:c_len // 2 + 1] if reverse else cc[c_len // 2 - 1:c_len // 2]
        last = cc[0:1] if reverse else cc[c_len - 1:c_len]
        qc = gqk_ref[0, r0:r0 + c_len, 0:GLA_QK_W] * GLA_SCALE
        kc = gqk_ref[0, r0:r0 + c_len, GLA_QK_W:2 * GLA_QK_W]
        vc = gv_ref[0, r0:r0 + c_len, :]
        rel = cc - mid
        q_in = qc * jnp.exp(rel)
        k_in = (kc * jnp.exp(-rel)).astype(BF16)
        k_up = kc * jnp.exp(last - cc)
        q_st = qc * jnp.exp(cc)
        qbd_in = jnp.concatenate([q_in * m for m in head_mask], axis=0).astype(BF16)
        qbd_st = jnp.concatenate([q_st * m for m in head_mask], axis=0).astype(BF16)
        scores = _dot_nt(qbd_in, k_in)
        p = jnp.where(keep, scores, 0.0).astype(BF16)
        state = state_ref[d]
        o_state = _dot(qbd_st, state.astype(BF16))
        xt = jnp.concatenate([k_up, jnp.broadcast_to(last, (c_len, GLA_QK_W))], axis=0).T
        k_up_t = xt[:, 0:c_len].astype(BF16)
        decay_col = jnp.exp(xt[:, c_len:c_len + 1])
        upd = []
        for h in range(GLA_HEADS):
            rows = slice(c_len * h, c_len * (h + 1))
            vh = vc[:, GLA_DV * h:GLA_DV * (h + 1)]
            out_ref[0, r0:r0 + c_len, GLA_DV * h:GLA_DV * (h + 1)] = _dot(p[rows], vh) + o_state[rows]
            upd.append(_dot(k_up_t[rows], vh))
        state_ref[d] = decay_col * state + jnp.concatenate(upd, axis=0)


def _kernel_gla(gqkf_ref, gvf_ref, zf_ref, gqkb_ref, gvb_ref, zb_ref,
                w2f_ref, b2f_ref, w2b_ref, b2b_ref, lower_ref, upper_ref,
                of_ref, ob_ref, state_ref):
    @pl.when(pl.program_id(1) == 0)
    def _():
        state_ref[...] = jnp.zeros_like(state_ref)

    _gla_block(gqkf_ref, gvf_ref, zf_ref, w2f_ref, b2f_ref, lower_ref, of_ref, state_ref, 0, False)
    _gla_block(gqkb_ref, gvb_ref, zb_ref, w2b_ref, b2b_ref, upper_ref, ob_ref, state_ref, 1, True)


def _call_gla(gqk, gv, z, w):
    bsz, n, _ = gqk.shape
    tn = GLA_TILE
    nb = n // tn
    fwd = lambda width: pl.BlockSpec((1, tn, width), lambda b, j: (b, j, 0))
    bwd = lambda width: pl.BlockSpec((1, tn, width), lambda b, j: (b, nb - 1 - j, 0))
    o_shape = jax.ShapeDtypeStruct((bsz, n, GLA_V_W), F32)
    return pl.pallas_call(
        _kernel_gla,
        grid=(bsz, nb),
        in_specs=[fwd(2 * GLA_QK_W), fwd(GLA_V_W), fwd(Z_W), bwd(2 * GLA_QK_W), bwd(GLA_V_W), bwd(Z_W),
                  _const_spec((Z_W, GLA_QK_W)), _const_spec((1, GLA_QK_W)),
                  _const_spec((Z_W, GLA_QK_W)), _const_spec((1, GLA_QK_W)),
                  _const_spec((CUM_TILE, CUM_TILE)), _const_spec((CUM_TILE, CUM_TILE))],
        out_specs=(fwd(GLA_V_W), bwd(GLA_V_W)),
        out_shape=(o_shape, o_shape),
        scratch_shapes=[pltpu.VMEM((2, GLA_HEADS * GLA_DK, GLA_DV), F32)],
        compiler_params=pltpu.CompilerParams(dimension_semantics=("arbitrary", "arbitrary"),
                                             vmem_limit_bytes=VMEM_LIMIT),
        name="gla_bidir",
    )(gqk, gv, z, gqk, gv, z, w["w2f"], w["b2f"], w["w2b"], w["b2b"], w["tri_lower"], w["tri_upper"])


def _kernel_attn(q_ref, k_ref, v_ref, o_ref):
    q = q_ref[0]
    k = k_ref[0, 0]
    v = v_ref[0, 0]
    outs = []
    for hh in range(LANES // ATTN_DH):
        s = _dot_nt(q[:, ATTN_DH * hh:ATTN_DH * (hh + 1)], k)
        m = jnp.max(s, axis=-1, keepdims=True)
        p = jnp.exp(s - m)
        l = jnp.sum(p, axis=-1, keepdims=True)
        outs.append(_dot(p.astype(BF16), v) / l)
    o_ref[0] = jnp.concatenate(outs, axis=1).astype(BF16)


def _call_attn(q, k, v):
    bsz, n, _ = q.shape
    tq = ATTN_Q_TILE
    pairs = ATTN_HEADS // ATTN_KV_HEADS // 2
    q_spec = pl.BlockSpec((1, tq, LANES), lambda b, g, i, pr: (b, i, pairs * g + pr))
    kv_spec = pl.BlockSpec((1, 1, n, ATTN_DH), lambda b, g, i, pr: (b, g, 0, 0))
    return pl.pallas_call(
        _kernel_attn,
        grid=(bsz, ATTN_KV_HEADS, n // tq, pairs),
        in_specs=[q_spec, kv_spec, kv_spec],
        out_specs=q_spec,
        out_shape=jax.ShapeDtypeStruct((bsz, n, ATTN_Q_W), BF16),
        compiler_params=pltpu.CompilerParams(
            dimension_semantics=("arbitrary", "arbitrary", "arbitrary", "arbitrary"),
            vmem_limit_bytes=VMEM_LIMIT),
        name="gqa_attention",
    )(q, k, v)


def _kernel_b(h1_ref, of_ref, ob_ref, gr_ref, oa_ref, p_ref, gn_ref, wog_ref, woa_ref,
              ln2g_ref, ln2b_ref, wg_ref, wu_ref, wd_ref, ln3g_ref, ln3b_ref, wpg_ref, bpg_ref, wpe_ref,
              out_ref):
    o = of_ref[0] + ob_ref[0]
    gn = gn_ref[...]
    normed = []
    for h in range(GLA_HEADS):
        oh = o[:, GLA_DV * h:GLA_DV * (h + 1)]
        ms = jnp.mean(oh * oh, axis=-1, keepdims=True)
        normed.append(oh * lax.rsqrt(ms + GN_EPS) * gn)
    gr = gr_ref[0]
    o_gla = jnp.concatenate(normed, axis=1) * (gr * jax.nn.sigmoid(gr))
    mix = _dot(o_gla.astype(BF16), wog_ref[...]) + _dot(oa_ref[0], woa_ref[...])
    h2 = _layer_norm(DEEPNORM_ALPHA * h1_ref[0] + mix, ln2g_ref[...], ln2b_ref[...])
    f = _swiglu(h2.astype(BF16), wg_ref, wu_ref, wd_ref)
    h3 = _layer_norm(DEEPNORM_ALPHA * h2 + 0.5 * f, ln3g_ref[...], ln3b_ref[...])
    gate = jax.nn.sigmoid(_dot(h3.astype(BF16), wpg_ref[...]) + bpg_ref[...])
    out_ref[0] = h3 + gate * _dot(p_ref[0].astype(BF16), wpe_ref[...])


def _call_b(h1, o_f, o_b, gr, o_att, p, w):
    bsz, n, _ = h1.shape
    tm = TOKEN_TILE
    tok = lambda width: pl.BlockSpec((1, tm, width), lambda b, i: (b, i, 0))
    return pl.pallas_call(
        _kernel_b,
        grid=(bsz, n // tm),
        in_specs=[tok(D_MODEL), tok(GLA_V_W), tok(GLA_V_W), tok(GLA_V_W), tok(ATTN_Q_W), tok(P_DIM),
                  _const_spec((1, GLA_DV)),
                  _const_spec((GLA_V_W, D_MODEL)), _const_spec((ATTN_Q_W, D_MODEL)),
                  _const_spec((1, D_MODEL)), _const_spec((1, D_MODEL)),
                  _const_spec((D_MODEL, D_FF)), _const_spec((D_MODEL, D_FF)), _const_spec((D_FF, D_MODEL)),
                  _const_spec((1, D_MODEL)), _const_spec((1, D_MODEL)),
                  _const_spec((D_MODEL, D_MODEL)), _const_spec((1, D_MODEL)), _const_spec((P_DIM, D_MODEL))],
        out_specs=tok(D_MODEL),
        out_shape=jax.ShapeDtypeStruct((bsz, n, D_MODEL), F32),
        compiler_params=pltpu.CompilerParams(dimension_semantics=("arbitrary", "arbitrary"),
                                             vmem_limit_bytes=VMEM_LIMIT),
        name="outproj_ffn2_embed",
    )(h1, o_f, o_b, gr, o_att, p, w["gn_g"], w["w_out_gla"], w["w_out_att"], w["ln2_g"], w["ln2_b"],
      w["ffn2_wg"], w["ffn2_wu"], w["ffn2_wd"], w["ln3_g"], w["ln3_b"], w["w_pg"], w["b_pg"], w["w_pe"])


def _rope_tables(n):
    t = jnp.arange(n, dtype=jnp.int32)
    row = (t // GRID_W).astype(F32)
    col = (t % GRID_W).astype(F32)
    axis_dim = ATTN_DH // 2
    inv_freq = ROPE_THETA ** (-jnp.arange(0, axis_dim, 2, dtype=F32) / axis_dim)
    lane = jnp.arange(LANES, dtype=jnp.int32) % ATTN_DH
    freq = inv_freq[lane % ROPE_HALF]
    pos = jnp.where((lane // axis_dim)[None, :] == 0, row[:, None], col[:, None])
    ang = pos * freq[None, :]
    first_half = ((lane % axis_dim) // ROPE_HALF == 0)[None, :]
    cos, sin = jnp.cos(ang), jnp.sin(ang)
    return cos, jnp.where(first_half, -sin, 0.0), jnp.where(first_half, 0.0, sin)


def _prepare(ffn1_wg, ffn1_wu, ffn1_wd, ln1_g, ln1_b, w_in, gla_w2f, gla_b2f, gla_w2b, gla_b2b,
             gla_gn_g, q_norm_g, k_norm_g, w_out, ln2_g, ln2_b, ffn2_wg, ffn2_wu, ffn2_wd,
             ln3_g, ln3_b, w_pg, b_pg, w_pe):
    row = lambda v: v.reshape(1, -1).astype(F32)
    z0 = 2 * GLA_QK_W + 2 * GLA_V_W
    w_in_r = jnp.concatenate([w_in[:, :z0], w_in[:, z0 + Z_W:], w_in[:, z0:z0 + Z_W]], axis=1)
    zeros = jnp.zeros((GLA_GATE_RANK, GLA_QK_W), F32)
    idx = jnp.arange(ATTN_Q_W)
    avg = jnp.where((idx[:, None] // ATTN_DH) == (idx[None, :] // ATTN_DH), 1.0 / ATTN_DH, 0.0)
    ci = jnp.arange(CUM_TILE)
    same_chunk = (ci[:, None] // GLA_CHUNK) == (ci[None, :] // GLA_CHUNK)
    lower = jnp.where(same_chunk & (ci[None, :] <= ci[:, None]), 1.0, 0.0)
    return {
        "ffn1_wg": ffn1_wg.astype(BF16), "ffn1_wu": ffn1_wu.astype(BF16), "ffn1_wd": ffn1_wd.astype(BF16),
        "ln1_g": row(ln1_g), "ln1_b": row(ln1_b),
        "w_in": w_in_r.astype(BF16), "avg": avg.astype(BF16),
        "q_gain": row(jnp.tile(q_norm_g, ATTN_HEADS)), "k_gain": row(jnp.tile(k_norm_g, ATTN_KV_HEADS)),
        "w2f": jnp.concatenate([gla_w2f, zeros], axis=0).astype(BF16), "b2f": row(gla_b2f),
        "w2b": jnp.concatenate([zeros, gla_w2b], axis=0).astype(BF16), "b2b": row(gla_b2b),
        "tri_lower": lower.astype(BF16), "tri_upper": lower.T.astype(BF16),
        "gn_g": row(gla_gn_g),
        "w_out_gla": w_out[:GLA_V_W].astype(BF16), "w_out_att": w_out[GLA_V_W:].astype(BF16),
        "ln2_g": row(ln2_g), "ln2_b": row(ln2_b),
        "ffn2_wg": ffn2_wg.astype(BF16), "ffn2_wu": ffn2_wu.astype(BF16), "ffn2_wd": ffn2_wd.astype(BF16),
        "ln3_g": row(ln3_g), "ln3_b": row(ln3_b),
        "w_pg": w_pg.astype(BF16), "b_pg": row(b_pg), "w_pe": w_pe.astype(BF16),
    }


def _encoder_layer(x, p, w):
    n = x.shape[1]
    assert n % GLA_TILE == 0 and n % ATTN_Q_TILE == 0 and n % TOKEN_TILE == 0 and n % GRID_W == 0
    h1, gqk, gv, gr, z, q, k, v = _call_a(x, w, _rope_tables(n))
    o_f, o_b = _call_gla(gqk, gv, z, w)
    o_att = _call_attn(q, k, v)
    return _call_b(h1, o_f, o_b, gr, o_att, p, w)


def kernel(x_prompt, x_sample, p_prompt, p_sample, ffn1_wg, ffn1_wu, ffn1_wd, ln1_g, ln1_b, w_in,
           gla_w2f, gla_b2f, gla_w2b, gla_b2b, gla_gn_g, q_norm_g, k_norm_g, w_out, ln2_g, ln2_b,
           ffn2_wg, ffn2_wu, ffn2_wd, ln3_g, ln3_b, w_pg, b_pg, w_pe):
    y_prompt, y_sample = x_prompt, x_sample
    for i in range(DEPTH):
        w = _prepare(ffn1_wg[i], ffn1_wu[i], ffn1_wd[i], ln1_g[i], ln1_b[i], w_in[i],
                     gla_w2f[i], gla_b2f[i], gla_w2b[i], gla_b2b[i], gla_gn_g[i], q_norm_g[i],
                     k_norm_g[i], w_out[i], ln2_g[i], ln2_b[i], ffn2_wg[i], ffn2_wu[i], ffn2_wd[i],
                     ln3_g[i], ln3_b[i], w_pg[i], b_pg[i], w_pe[i])
        y_prompt = _encoder_layer(y_prompt, p_prompt[i], w)
        y_sample = _encoder_layer(y_sample, p_sample[i], w)
    return (y_prompt, y_sample)
```

```python
import functools

import jax
import jax.numpy as jnp
from jax import lax
from jax.experimental import pallas as pl
from jax.experimental.pallas import tpu as pltpu

F32 = jnp.float32
BF16 = jnp.bfloat16

D_MODEL = 1024
D_FF = 2816
P_DIM = 256
GRID_W = 64
DEPTH = 1
GLA_HEADS = 4
GLA_DK = 64
GLA_DV = 128
GLA_GATE_RANK = 16
GLA_TAU = 16.0
GLA_CHUNK = 64
ATTN_HEADS = 8
ATTN_KV_HEADS = 2
ATTN_DH = 64
ROPE_THETA = 10000.0
LN_EPS = 1e-5
QK_EPS = 1e-6
GN_EPS = 1e-5
DEEPNORM_ALPHA = (2.0 * DEPTH) ** 0.25

GLA_QK_W = GLA_HEADS * GLA_DK
GLA_V_W = GLA_HEADS * GLA_DV
ATTN_Q_W = ATTN_HEADS * ATTN_DH
ATTN_KV_W = ATTN_KV_HEADS * ATTN_DH
Z_W = 2 * GLA_GATE_RANK
OFF_GQK = 0
OFF_GV = OFF_GQK + 2 * GLA_QK_W
OFF_GR = OFF_GV + GLA_V_W
OFF_AQ = OFF_GR + GLA_V_W
OFF_AK = OFF_AQ + ATTN_Q_W
OFF_AV = OFF_AK + ATTN_KV_W
OFF_Z = OFF_AV + ATTN_KV_W
D_IN = OFF_Z + Z_W

LANES = 128
ROPE_HALF = ATTN_DH // 4
QK_SCALE = ATTN_DH ** -0.5 * 1.4426950408889634
GLA_SCALE = GLA_DK ** -0.5

TOKEN_TILE = 256
GLA_TILE = 512
CUM_TILE = 256
ATTN_Q_TILE = 512
ATTN_UNIT_ROWS = 256
ATTN_KEY_TILE = 512
VT_ROWS = ATTN_DH + 16
VMEM_LIMIT = 56 * 1024 * 1024


def _const_spec(shape):
    return pl.BlockSpec(shape, lambda *_: (0,) * len(shape), pipeline_mode=pl.Buffered(1))


def _dot(a, b):
    return jnp.dot(a, b, preferred_element_type=F32)


def _dot_nt(a, b):
    return lax.dot_general(a, b, (((1,), (1,)), ((), ())), preferred_element_type=F32)


def _split_bf16(x):
    hi = x.astype(BF16)
    lo = (x - hi.astype(F32)).astype(BF16)
    return hi, lo


def _layer_norm(y, g, b):
    mu = jnp.mean(y, axis=-1, keepdims=True)
    d = y - mu
    var = jnp.mean(d * d, axis=-1, keepdims=True)
    return d * lax.rsqrt(var + LN_EPS) * g + b


def _swiglu(xb, wg_ref, wu_ref, wd_ref):
    g = _dot(xb, wg_ref[...])
    u = _dot(xb, wu_ref[...])
    hid = (g * jax.nn.sigmoid(g)) * u
    return _dot(hid.astype(BF16), wd_ref[...])


def _head_rms(x, avg, gain, eps):
    hi, lo = _split_bf16(x * x)
    ms = _dot(hi, avg) + _dot(lo, avg)
    return x * lax.rsqrt(ms + eps) * gain


def _rope(xs, c, sa, sb):
    return (xs * c + pltpu.roll(xs, LANES - ROPE_HALF, 1) * sa
            + pltpu.roll(xs, ROPE_HALF, 1) * sb)


def _kernel_a(x_ref, wg_ref, wu_ref, wd_ref, lng_ref, lnb_ref, win_ref, avg_ref, qg_ref, kg_ref,
              c_ref, sa_ref, sb_ref,
              h_ref, gqk_ref, gv_ref, gr_ref, z_ref, q_ref, k_ref, v_ref):
    x = x_ref[0]
    f = _swiglu(x.astype(BF16), wg_ref, wu_ref, wd_ref)
    h = _layer_norm(DEEPNORM_ALPHA * x + 0.5 * f, lng_ref[...], lnb_ref[...])
    h_ref[0] = h
    proj = _dot(h.astype(BF16), win_ref[...])
    gqk_ref[0] = proj[:, OFF_GQK:OFF_GV]
    gv_ref[0] = proj[:, OFF_GV:OFF_GR].astype(BF16)
    gr_ref[0] = proj[:, OFF_GR:OFF_AQ]
    z_ref[0] = proj[:, OFF_Z:D_IN]
    c, sa, sb = c_ref[...], sa_ref[...], sb_ref[...]
    avg = avg_ref[...]
    qn = _head_rms(proj[:, OFF_AQ:OFF_AK], avg, qg_ref[...], QK_EPS)
    for j in range(ATTN_Q_W // LANES):
        sl = slice(LANES * j, LANES * (j + 1))
        q_ref[0, :, sl] = (_rope(qn[:, sl], c, sa, sb) * QK_SCALE).astype(BF16)
    kn = _head_rms(proj[:, OFF_AK:OFF_AV], avg[:ATTN_KV_W, :ATTN_KV_W], kg_ref[...], QK_EPS)
    kr = _rope(kn, c, sa, sb).astype(BF16)
    vt = proj[:, OFF_AV:OFF_Z].T.astype(BF16)
    ones = jnp.ones((VT_ROWS - ATTN_DH, vt.shape[1]), BF16)
    for g in range(ATTN_KV_HEADS):
        k_ref[0, g] = kr[:, ATTN_DH * g:ATTN_DH * (g + 1)]
        v_ref[0, g, 0:ATTN_DH, :] = vt[ATTN_DH * g:ATTN_DH * (g + 1), :]
        v_ref[0, g, ATTN_DH:VT_ROWS, :] = ones


def _call_a(x, w, tabs):
    bsz, n, _ = x.shape
    tm = TOKEN_TILE
    tok = lambda width: pl.BlockSpec((1, tm, width), lambda b, i: (b, i, 0))
    tab = pl.BlockSpec((tm, LANES), lambda b, i: (i, 0))
    kv_spec = pl.BlockSpec((1, ATTN_KV_HEADS, tm, ATTN_DH), lambda b, i: (b, 0, i, 0))
    vt_spec = pl.BlockSpec((1, ATTN_KV_HEADS, VT_ROWS, tm), lambda b, i: (b, 0, 0, i))
    out_shape = (
        jax.ShapeDtypeStruct((bsz, n, D_MODEL), F32),
        jax.ShapeDtypeStruct((bsz, n, 2 * GLA_QK_W), F32),
        jax.ShapeDtypeStruct((bsz, n, GLA_V_W), BF16),
        jax.ShapeDtypeStruct((bsz, n, GLA_V_W), F32),
        jax.ShapeDtypeStruct((bsz, n, Z_W), F32),
        jax.ShapeDtypeStruct((bsz, n, ATTN_Q_W), BF16),
        jax.ShapeDtypeStruct((bsz, ATTN_KV_HEADS, n, ATTN_DH), BF16),
        jax.ShapeDtypeStruct((bsz, ATTN_KV_HEADS, VT_ROWS, n), BF16),
    )
    return pl.pallas_call(
        _kernel_a,
        grid=(bsz, n // tm),
        in_specs=[tok(D_MODEL),
                  _const_spec((D_MODEL, D_FF)), _const_spec((D_MODEL, D_FF)), _const_spec((D_FF, D_MODEL)),
                  _const_spec((1, D_MODEL)), _const_spec((1, D_MODEL)),
                  _const_spec((D_MODEL, D_IN)), _const_spec((ATTN_Q_W, ATTN_Q_W)),
                  _const_spec((1, ATTN_Q_W)), _const_spec((1, ATTN_KV_W)),
                  tab, tab, tab],
        out_specs=(tok(D_MODEL), tok(2 * GLA_QK_W), tok(GLA_V_W), tok(GLA_V_W), tok(Z_W), tok(ATTN_Q_W),
                   kv_spec, vt_spec),
        out_shape=out_shape,
        compiler_params=pltpu.CompilerParams(dimension_semantics=("arbitrary", "arbitrary"),
                                             vmem_limit_bytes=VMEM_LIMIT),
        name="ffn1_inproj",
    )(x, w["ffn1_wg"], w["ffn1_wu"], w["ffn1_wd"], w["ln1_g"], w["ln1_b"], w["w_in"], w["avg"],
      w["q_gain"], w["k_gain"], *tabs)


def _gla_block(gqk_ref, gv_ref, z_ref, w2_ref, b2_ref, tri_ref, out_ref, state_ref, d, reverse):
    tn = gqk_ref.shape[1]
    c_len = GLA_CHUNK
    pre = _dot(z_ref[0].astype(BF16), w2_ref[...]) + b2_ref[...]
    log_a = (jnp.minimum(pre, 0.0) - jnp.log1p(jnp.exp(-jnp.abs(pre)))) * (1.0 / GLA_TAU)
    hi, lo = _split_bf16(log_a)
    tri = tri_ref[...]
    cum = jnp.concatenate(
        [_dot(tri, hi[r:r + CUM_TILE]) + _dot(tri, lo[r:r + CUM_TILE]) for r in range(0, tn, CUM_TILE)],
        axis=0)

    lane_head = lax.broadcasted_iota(jnp.int32, (c_len, GLA_QK_W), 1) // GLA_DK
    head_mask = [(lane_head == h).astype(F32) for h in range(GLA_HEADS)]
    row = lax.broadcasted_iota(jnp.int32, (GLA_HEADS * c_len, c_len), 0) % c_len
    col = lax.broadcasted_iota(jnp.int32, (GLA_HEADS * c_len, c_len), 1)
    keep = (col > row) if reverse else (col <= row)

    n_chunks = tn // c_len
    for c in (range(n_chunks - 1, -1, -1) if reverse else range(n_chunks)):
        r0 = c_len * c
        cc = cum[r0:r0 + c_len]
        mid = cc[c_len // 2:c_len // 2 + 1] if reverse else cc[c_len // 2 - 1:c_len // 2]
        last = cc[0:1] if reverse else cc[c_len - 1:c_len]
        qc = gqk_ref[0, r0:r0 + c_len, 0:GLA_QK_W] * GLA_SCALE
        kc = gqk_ref[0, r0:r0 + c_len, GLA_QK_W:2 * GLA_QK_W]
        vc = gv_ref[0, r0:r0 + c_len, :]
        rel = cc - mid
        q_in = qc * jnp.exp(rel)
        k_in = (kc * jnp.exp(-rel)).astype(BF16)
        k_up = kc * jnp.exp(last - cc)
        q_st = qc * jnp.exp(cc)
        qbd_in = jnp.concatenate([q_in * m for m in head_mask], axis=0).astype(BF16)
        qbd_st = jnp.concatenate([q_st * m for m in head_mask], axis=0).astype(BF16)
        scores = _dot_nt(qbd_in, k_in)
        p = jnp.where(keep, scores, 0.0).astype(BF16)
        state = state_ref[d]
        o_state = _dot(qbd_st, state.astype(BF16))
        xt = jnp.concatenate([k_up, jnp.broadcast_to(last, (c_len, GLA_QK_W))], axis=0).T
        k_up_t = xt[:, 0:c_len].astype(BF16)
        decay_col = jnp.exp(xt[:, c_len:c_len + 1])
        upd = []
        for h in range(GLA_HEADS):
            rows = slice(c_len * h, c_len * (h + 1))
            vh = vc[:, GLA_DV * h:GLA_DV * (h + 1)]
            out_ref[0, r0:r0 + c_len, GLA_DV * h:GLA_DV * (h + 1)] = _dot(p[rows], vh) + o_state[rows]
            upd.append(_dot(k_up_t[rows], vh))
        state_ref[d] = decay_col * state + jnp.concatenate(upd, axis=0)


def _kernel_gla(gqkf_ref, gvf_ref, zf_ref, gqkb_ref, gvb_ref, zb_ref,
                w2f_ref, b2f_ref, w2b_ref, b2b_ref, lower_ref, upper_ref,
                of_ref, ob_ref, state_ref):
    @pl.when(pl.program_id(1) == 0)
    def _():
        state_ref[...] = jnp.zeros_like(state_ref)

    _gla_block(gqkf_ref, gvf_ref, zf_ref, w2f_ref, b2f_ref, lower_ref, of_ref, state_ref, 0, False)
    _gla_block(gqkb_ref, gvb_ref, zb_ref, w2b_ref, b2b_ref, upper_ref, ob_ref, state_ref, 1, True)


def _call_gla(gqk, gv, z, w):
    bsz, n, _ = gqk.shape
    tn = GLA_TILE
    nb = n // tn
    fwd = lambda width: pl.BlockSpec((1, tn, width), lambda b, j: (b, j, 0))
    bwd = lambda width: pl.BlockSpec((1, tn, width), lambda b, j: (b, nb - 1 - j, 0))
    o_shape = jax.ShapeDtypeStruct((bsz, n, GLA_V_W), F32)
    return pl.pallas_call(
        _kernel_gla,
        grid=(bsz, nb),
        in_specs=[fwd(2 * GLA_QK_W), fwd(GLA_V_W), fwd(Z_W), bwd(2 * GLA_QK_W), bwd(GLA_V_W), bwd(Z_W),
                  _const_spec((Z_W, GLA_QK_W)), _const_spec((1, GLA_QK_W)),
                  _const_spec((Z_W, GLA_QK_W)), _const_spec((1, GLA_QK_W)),
                  _const_spec((CUM_TILE, CUM_TILE)), _const_spec((CUM_TILE, CUM_TILE))],
        out_specs=(fwd(GLA_V_W), bwd(GLA_V_W)),
        out_shape=(o_shape, o_shape),
        scratch_shapes=[pltpu.VMEM((2, GLA_HEADS * GLA_DK, GLA_DV), F32)],
        compiler_params=pltpu.CompilerParams(dimension_semantics=("arbitrary", "arbitrary"),
                                             vmem_limit_bytes=VMEM_LIMIT),
        name="gla_bidir",
    )(gqk, gv, z, gqk, gv, z, w["w2f"], w["b2f"], w["w2b"], w["b2b"], w["tri_lower"], w["tri_upper"])


def _kernel_attn(q_ref, k_ref, vt_ref, o_ref, st_ref, pt_ref):
    n = k_ref.shape[2]
    sub = ATTN_UNIT_ROWS
    units = [(s, pair) for s in range(q_ref.shape[1] // sub) for pair in range(ATTN_HEADS // ATTN_KV_HEADS // 2)]
    n_units = len(units)
    tiles = [slice(t, t + ATTN_KEY_TILE) for t in range(0, n, ATTN_KEY_TILE)]

    def unit_q(u):
        s, pair = units[u]
        rows = slice(sub * s, sub * (s + 1))
        heads = [q_ref[0, rows, ATTN_DH * h:ATTN_DH * (h + 1)] for h in (2 * pair, 2 * pair + 1)]
        return jnp.concatenate(heads, axis=0)

    col_max = [None] * n_units
    for phase in range(n_units + 2):
        u1, u2, u3 = phase, phase - 1, phase - 2
        q_cur = unit_q(u1) if u1 < n_units else None
        m_run = None
        acc = jnp.zeros((VT_ROWS, 2 * sub), F32)
        for tile in tiles:
            if u1 < n_units:
                st = _dot_nt(k_ref[0, 0, tile, :], q_cur)
                st_ref[u1 % 2, tile, :] = st
                m_tile = jnp.max(st, axis=0, keepdims=True)
                m_run = m_tile if m_run is None else jnp.maximum(m_run, m_tile)
            if 0 <= u2 < n_units:
                pt_ref[u2 % 2, tile, :] = jnp.exp2(st_ref[u2 % 2, tile, :] - col_max[u2]).astype(BF16)
            if 0 <= u3:
                acc = acc + _dot(vt_ref[0, 0, :, tile], pt_ref[u3 % 2, tile, :])
        if u1 < n_units:
            col_max[u1] = m_run
        if 0 <= u3:
            s, pair = units[u3]
            o2 = (acc[0:ATTN_DH] / acc[ATTN_DH:ATTN_DH + 1]).T
            o_ref[0, sub * s:sub * (s + 1), LANES * pair:LANES * (pair + 1)] = (
                jnp.concatenate([o2[:sub], o2[sub:]], axis=1).astype(BF16))


def _call_attn(q, k, vt):
    bsz, n, _ = q.shape
    tq = ATTN_Q_TILE
    group_w = ATTN_Q_W // ATTN_KV_HEADS
    q_spec = pl.BlockSpec((1, tq, group_w), lambda b, g, i: (b, i, g))
    k_spec = pl.BlockSpec((1, 1, n, ATTN_DH), lambda b, g, i: (b, g, 0, 0))
    vt_spec = pl.BlockSpec((1, 1, VT_ROWS, n), lambda b, g, i: (b, g, 0, 0))
    return pl.pallas_call(
        _kernel_attn,
        grid=(bsz, ATTN_KV_HEADS, n // tq),
        in_specs=[q_spec, k_spec, vt_spec],
        out_specs=q_spec,
        out_shape=jax.ShapeDtypeStruct((bsz, n, ATTN_Q_W), BF16),
        scratch_shapes=[pltpu.VMEM((2, n, 2 * ATTN_UNIT_ROWS), F32),
                        pltpu.VMEM((2, n, 2 * ATTN_UNIT_ROWS), BF16)],
        compiler_params=pltpu.CompilerParams(
            dimension_semantics=("arbitrary", "arbitrary", "arbitrary"),
            vmem_limit_bytes=VMEM_LIMIT),
        name="gqa_attention",
    )(q, k, vt)


def _kernel_b(h1_ref, of_ref, ob_ref, gr_ref, oa_ref, p_ref, gn_ref, wog_ref, woa_ref,
              ln2g_ref, ln2b_ref, wg_ref, wu_ref, wd_ref, ln3g_ref, ln3b_ref, wpg_ref, bpg_ref, wpe_ref,
              out_ref):
    o = of_ref[0] + ob_ref[0]
    gn = gn_ref[...]
    normed = []
    for h in range(GLA_HEADS):
        oh = o[:, GLA_DV * h:GLA_DV * (h + 1)]
        ms = jnp.mean(oh * oh, axis=-1, keepdims=True)
        normed.append(oh * lax.rsqrt(ms + GN_EPS) * gn)
    gr = gr_ref[0]
    o_gla = jnp.concatenate(normed, axis=1) * (gr * jax.nn.sigmoid(gr))
    mix = _dot(o_gla.astype(BF16), wog_ref[...]) + _dot(oa_ref[0], woa_ref[...])
    h2 = _layer_norm(DEEPNORM_ALPHA * h1_ref[0] + mix, ln2g_ref[...], ln2b_ref[...])
    f = _swiglu(h2.astype(BF16), wg_ref, wu_ref, wd_ref)
    h3 = _layer_norm(DEEPNORM_ALPHA * h2 + 0.5 * f, ln3g_ref[...], ln3b_ref[...])
    gate = jax.nn.sigmoid(_dot(h3.astype(BF16), wpg_ref[...]) + bpg_ref[...])
    out_ref[0] = h3 + gate * _dot(p_ref[0].astype(BF16), wpe_ref[...])


def _call_b(h1, o_f, o_b, gr, o_att, p, w):
    bsz, n, _ = h1.shape
    tm = TOKEN_TILE
    tok = lambda width: pl.BlockSpec((1, tm, width), lambda b, i: (b, i, 0))
    return pl.pallas_call(
        _kernel_b,
        grid=(bsz, n // tm),
        in_specs=[tok(D_MODEL), tok(GLA_V_W), tok(GLA_V_W), tok(GLA_V_W), tok(ATTN_Q_W), tok(P_DIM),
                  _const_spec((1, GLA_DV)),
                  _const_spec((GLA_V_W, D_MODEL)), _const_spec((ATTN_Q_W, D_MODEL)),
                  _const_spec((1, D_MODEL)), _const_spec((1, D_MODEL)),
                  _const_spec((D_MODEL, D_FF)), _const_spec((D_MODEL, D_FF)), _const_spec((D_FF, D_MODEL)),
                  _const_spec((1, D_MODEL)), _const_spec((1, D_MODEL)),
                  _const_spec((D_MODEL, D_MODEL)), _const_spec((1, D_MODEL)), _const_spec((P_DIM, D_MODEL))],
        out_specs=tok(D_MODEL),
        out_shape=jax.ShapeDtypeStruct((bsz, n, D_MODEL), F32),
        compiler_params=pltpu.CompilerParams(dimension_semantics=("arbitrary", "arbitrary"),
                                             vmem_limit_bytes=VMEM_LIMIT),
        name="outproj_ffn2_embed",
    )(h1, o_f, o_b, gr, o_att, p, w["gn_g"], w["w_out_gla"], w["w_out_att"], w["ln2_g"], w["ln2_b"],
      w["ffn2_wg"], w["ffn2_wu"], w["ffn2_wd"], w["ln3_g"], w["ln3_b"], w["w_pg"], w["b_pg"], w["w_pe"])


def _rope_tables(n):
    t = jnp.arange(n, dtype=jnp.int32)
    row = (t // GRID_W).astype(F32)
    col = (t % GRID_W).astype(F32)
    axis_dim = ATTN_DH // 2
    inv_freq = ROPE_THETA ** (-jnp.arange(0, axis_dim, 2, dtype=F32) / axis_dim)
    lane = jnp.arange(LANES, dtype=jnp.int32) % ATTN_DH
    freq = inv_freq[lane % ROPE_HALF]
    pos = jnp.where((lane // axis_dim)[None, :] == 0, row[:, None], col[:, None])
    ang = pos * freq[None, :]
    first_half = ((lane % axis_dim) // ROPE_HALF == 0)[None, :]
    cos, sin = jnp.cos(ang), jnp.sin(ang)
    return cos, jnp.where(first_half, -sin, 0.0), jnp.where(first_half, 0.0, sin)


def _prepare(ffn1_wg, ffn1_wu, ffn1_wd, ln1_g, ln1_b, w_in, gla_w2f, gla_b2f, gla_w2b, gla_b2b,
             gla_gn_g, q_norm_g, k_norm_g, w_out, ln2_g, ln2_b, ffn2_wg, ffn2_wu, ffn2_wd,
             ln3_g, ln3_b, w_pg, b_pg, w_pe):
    row = lambda v: v.reshape(1, -1).astype(F32)
    z0 = 2 * GLA_QK_W + 2 * GLA_V_W
    w_in_r = jnp.concatenate([w_in[:, :z0], w_in[:, z0 + Z_W:], w_in[:, z0:z0 + Z_W]], axis=1)
    zeros = jnp.zeros((GLA_GATE_RANK, GLA_QK_W), F32)
    idx = jnp.arange(ATTN_Q_W)
    avg = jnp.where((idx[:, None] // ATTN_DH) == (idx[None, :] // ATTN_DH), 1.0 / ATTN_DH, 0.0)
    ci = jnp.arange(CUM_TILE)
    same_chunk = (ci[:, None] // GLA_CHUNK) == (ci[None, :] // GLA_CHUNK)
    lower = jnp.where(same_chunk & (ci[None, :] <= ci[:, None]), 1.0, 0.0)
    return {
        "ffn1_wg": ffn1_wg.astype(BF16), "ffn1_wu": ffn1_wu.astype(BF16), "ffn1_wd": ffn1_wd.astype(BF16),
        "ln1_g": row(ln1_g), "ln1_b": row(ln1_b),
        "w_in": w_in_r.astype(BF16), "avg": avg.astype(BF16),
        "q_gain": row(jnp.tile(q_norm_g, ATTN_HEADS)), "k_gain": row(jnp.tile(k_norm_g, ATTN_KV_HEADS)),
        "w2f": jnp.concatenate([gla_w2f, zeros], axis=0).astype(BF16), "b2f": row(gla_b2f),
        "w2b": jnp.concatenate([zeros, gla_w2b], axis=0).astype(BF16), "b2b": row(gla_b2b),
        "tri_lower": lower.astype(BF16), "tri_upper": lower.T.astype(BF16),
        "gn_g": row(gla_gn_g),
        "w_out_gla": w_out[:GLA_V_W].astype(BF16), "w_out_att": w_out[GLA_V_W:].astype(BF16),
        "ln2_g": row(ln2_g), "ln2_b": row(ln2_b),
        "ffn2_wg": ffn2_wg.astype(BF16), "ffn2_wu": ffn2_wu.astype(BF16), "ffn2_wd": ffn2_wd.astype(BF16),
        "ln3_g": row(ln3_g), "ln3_b": row(ln3_b),
        "w_pg": w_pg.astype(BF16), "b_pg": row(b_pg), "w_pe": w_pe.astype(BF16),
    }


def _encoder_layer(x, p, w):
    n = x.shape[1]
    assert n % GLA_TILE == 0 and n % ATTN_Q_TILE == 0 and n % TOKEN_TILE == 0 and n % GRID_W == 0
    h1, gqk, gv, gr, z, q, k, v = _call_a(x, w, _rope_tables(n))
    o_f, o_b = _call_gla(gqk, gv, z, w)
    o_att = _call_attn(q, k, v)
    return _call_b(h1, o_f, o_b, gr, o_att, p, w)


def kernel(x_prompt, x_sample, p_prompt, p_sample, ffn1_wg, ffn1_wu, ffn1_wd, ln1_g, ln1_b, w_in,
           gla_w2f, gla_b2f, gla_w2b, gla_b2b, gla_gn_g, q_norm_g, k_norm_g, w_out, ln2_g, ln2_b,
           ffn2_wg, ffn2_wu, ffn2_wd, ln3_g, ln3_b, w_pg, b_pg, w_pe):
    y_prompt, y_sample = x_prompt, x_sample
    for i in range(DEPTH):
        w = _prepare(ffn1_wg[i], ffn1_wu[i], ffn1_wd[i], ln1_g[i], ln1_b[i], w_in[i],
                     gla_w2f[i], gla_b2f[i], gla_w2b[i], gla_b2b[i], gla_gn_g[i], q_norm_g[i],
                     k_norm_g[i], w_out[i], ln2_g[i], ln2_b[i], ffn2_wg[i], ffn2_wu[i], ffn2_wd[i],
                     ln3_g[i], ln3_b[i], w_pg[i], b_pg[i], w_pe[i])
        y_prompt = _encoder_layer(y_prompt, p_prompt[i], w)
        y_sample = _encoder_layer(y_sample, p_sample[i], w)
    return (y_prompt, y_sample)
```

```python
import functools

import jax
import jax.numpy as jnp
from jax import lax
from jax.experimental import pallas as pl
from jax.experimental.pallas import tpu as pltpu

F32 = jnp.float32
BF16 = jnp.bfloat16

D_MODEL = 1024
D_FF = 2816
P_DIM = 256
GRID_W = 64
DEPTH = 1
GLA_HEADS = 4
GLA_DK = 64
GLA_DV = 128
GLA_GATE_RANK = 16
GLA_TAU = 16.0
GLA_CHUNK = 64
ATTN_HEADS = 8
ATTN_KV_HEADS = 2
ATTN_DH = 64
ROPE_THETA = 10000.0
LN_EPS = 1e-5
QK_EPS = 1e-6
GN_EPS = 1e-5
DEEPNORM_ALPHA = (2.0 * DEPTH) ** 0.25

GLA_QK_W = GLA_HEADS * GLA_DK
GLA_V_W = GLA_HEADS * GLA_DV
ATTN_Q_W = ATTN_HEADS * ATTN_DH
ATTN_KV_W = ATTN_KV_HEADS * ATTN_DH
Z_W = 2 * GLA_GATE_RANK
OFF_GQK = 0
OFF_GV = OFF_GQK + 2 * GLA_QK_W
OFF_GR = OFF_GV + GLA_V_W
OFF_AQ = OFF_GR + GLA_V_W
OFF_AK = OFF_AQ + ATTN_Q_W
OFF_AV = OFF_AK + ATTN_KV_W
OFF_Z = OFF_AV + ATTN_KV_W
D_IN = OFF_Z + Z_W

LANES = 128
ROPE_HALF = ATTN_DH // 4
QK_SCALE = ATTN_DH ** -0.5 * 1.4426950408889634
GLA_SCALE = GLA_DK ** -0.5

TOKEN_TILE = 256
GLA_TILE = 1024
CUM_TILE = 256
ATTN_Q_TILE = 1024
ATTN_UNIT_ROWS = 256
ATTN_KEY_TILE = 512
VT_ROWS = ATTN_DH + 16
VMEM_LIMIT = 56 * 1024 * 1024


def _const_spec(shape):
    return pl.BlockSpec(shape, lambda *_: (0,) * len(shape), pipeline_mode=pl.Buffered(1))


def _dot(a, b):
    return jnp.dot(a, b, preferred_element_type=F32)


def _dot_nt(a, b):
    return lax.dot_general(a, b, (((1,), (1,)), ((), ())), preferred_element_type=F32)


def _split_bf16(x):
    hi = x.astype(BF16)
    lo = (x - hi.astype(F32)).astype(BF16)
    return hi, lo


def _layer_norm(y, g, b):
    mu = jnp.mean(y, axis=-1, keepdims=True)
    d = y - mu
    var = jnp.mean(d * d, axis=-1, keepdims=True)
    return d * lax.rsqrt(var + LN_EPS) * g + b


def _swiglu(xb, wg_ref, wu_ref, wd_ref):
    g = _dot(xb, wg_ref[...])
    u = _dot(xb, wu_ref[...])
    hid = (g * jax.nn.sigmoid(g)) * u
    return _dot(hid.astype(BF16), wd_ref[...])


def _head_rms(x, avg, gain, eps):
    hi, lo = _split_bf16(x * x)
    ms = _dot(hi, avg) + _dot(lo, avg)
    return x * lax.rsqrt(ms + eps) * gain


def _rope(xs, c, sa, sb):
    return (xs * c + pltpu.roll(xs, LANES - ROPE_HALF, 1) * sa
            + pltpu.roll(xs, ROPE_HALF, 1) * sb)


def _kernel_a(x_ref, wg_ref, wu_ref, wd_ref, lng_ref, lnb_ref, win_ref, avg_ref, qg_ref, kg_ref,
              c_ref, sa_ref, sb_ref,
              h_ref, gqk_ref, gv_ref, gr_ref, z_ref, q_ref, k_ref, v_ref):
    x = x_ref[0]
    f = _swiglu(x.astype(BF16), wg_ref, wu_ref, wd_ref)
    h = _layer_norm(DEEPNORM_ALPHA * x + 0.5 * f, lng_ref[...], lnb_ref[...])
    h_ref[0] = h
    proj = _dot(h.astype(BF16), win_ref[...])
    gqk_ref[0] = proj[:, OFF_GQK:OFF_GV]
    gv_ref[0] = proj[:, OFF_GV:OFF_GR].astype(BF16)
    gr_ref[0] = proj[:, OFF_GR:OFF_AQ]
    z_ref[0] = proj[:, OFF_Z:D_IN]
    c, sa, sb = c_ref[...], sa_ref[...], sb_ref[...]
    avg = avg_ref[...]
    qn = _head_rms(proj[:, OFF_AQ:OFF_AK], avg, qg_ref[...], QK_EPS)
    for j in range(ATTN_Q_W // LANES):
        sl = slice(LANES * j, LANES * (j + 1))
        q_ref[0, :, sl] = (_rope(qn[:, sl], c, sa, sb) * QK_SCALE).astype(BF16)
    kn = _head_rms(proj[:, OFF_AK:OFF_AV], avg[:ATTN_KV_W, :ATTN_KV_W], kg_ref[...], QK_EPS)
    kr = _rope(kn, c, sa, sb).astype(BF16)
    vt = proj[:, OFF_AV:OFF_Z].T.astype(BF16)
    ones = jnp.ones((VT_ROWS - ATTN_DH, vt.shape[1]), BF16)
    for g in range(ATTN_KV_HEADS):
        k_ref[0, g] = kr[:, ATTN_DH * g:ATTN_DH * (g + 1)]
        v_ref[0, g, 0:ATTN_DH, :] = vt[ATTN_DH * g:ATTN_DH * (g + 1), :]
        v_ref[0, g, ATTN_DH:VT_ROWS, :] = ones


def _call_a(x, w, tabs):
    bsz, n, _ = x.shape
    tm = TOKEN_TILE
    tok = lambda width: pl.BlockSpec((1, tm, width), lambda b, i: (b, i, 0))
    tab = pl.BlockSpec((tm, LANES), lambda b, i: (i, 0))
    kv_spec = pl.BlockSpec((1, ATTN_KV_HEADS, tm, ATTN_DH), lambda b, i: (b, 0, i, 0))
    vt_spec = pl.BlockSpec((1, ATTN_KV_HEADS, VT_ROWS, tm), lambda b, i: (b, 0, 0, i))
    out_shape = (
        jax.ShapeDtypeStruct((bsz, n, D_MODEL), F32),
        jax.ShapeDtypeStruct((bsz, n, 2 * GLA_QK_W), F32),
        jax.ShapeDtypeStruct((bsz, n, GLA_V_W), BF16),
        jax.ShapeDtypeStruct((bsz, n, GLA_V_W), F32),
        jax.ShapeDtypeStruct((bsz, n, Z_W), F32),
        jax.ShapeDtypeStruct((bsz, n, ATTN_Q_W), BF16),
        jax.ShapeDtypeStruct((bsz, ATTN_KV_HEADS, n, ATTN_DH), BF16),
        jax.ShapeDtypeStruct((bsz, ATTN_KV_HEADS, VT_ROWS, n), BF16),
    )
    return pl.pallas_call(
        _kernel_a,
        grid=(bsz, n // tm),
        in_specs=[tok(D_MODEL),
                  _const_spec((D_MODEL, D_FF)), _const_spec((D_MODEL, D_FF)), _const_spec((D_FF, D_MODEL)),
                  _const_spec((1, D_MODEL)), _const_spec((1, D_MODEL)),
                  _const_spec((D_MODEL, D_IN)), _const_spec((ATTN_Q_W, ATTN_Q_W)),
                  _const_spec((1, ATTN_Q_W)), _const_spec((1, ATTN_KV_W)),
                  tab, tab, tab],
        out_specs=(tok(D_MODEL), tok(2 * GLA_QK_W), tok(GLA_V_W), tok(GLA_V_W), tok(Z_W), tok(ATTN_Q_W),
                   kv_spec, vt_spec),
        out_shape=out_shape,
        compiler_params=pltpu.CompilerParams(dimension_semantics=("arbitrary", "arbitrary"),
                                             vmem_limit_bytes=VMEM_LIMIT),
        name="ffn1_inproj",
    )(x, w["ffn1_wg"], w["ffn1_wu"], w["ffn1_wd"], w["ln1_g"], w["ln1_b"], w["w_in"], w["avg"],
      w["q_gain"], w["k_gain"], *tabs)


def _gla_log_decay_cumsum(z_ref, w2_ref, b2_ref, tri_ref):
    tn = z_ref.shape[1]
    pre = _dot(z_ref[0].astype(BF16), w2_ref[...]) + b2_ref[...]
    log_a = (jnp.minimum(pre, 0.0) - jnp.log(1.0 + jnp.exp(-jnp.abs(pre)))) * (1.0 / GLA_TAU)
    hi, lo = _split_bf16(log_a)
    tri = tri_ref[...]
    return jnp.concatenate(
        [_dot(tri, hi[r:r + CUM_TILE]) + _dot(tri, lo[r:r + CUM_TILE]) for r in range(0, tn, CUM_TILE)],
        axis=0)


def _gla_chunk(gqk_ref, gv_ref, out_ref, cum, state, c, head_mask, keep, reverse):
    c_len = GLA_CHUNK
    r0 = c_len * c
    cc = cum[r0:r0 + c_len]
    mid = cc[c_len // 2:c_len // 2 + 1] if reverse else cc[c_len // 2 - 1:c_len // 2]
    last = cc[0:1] if reverse else cc[c_len - 1:c_len]
    qc = gqk_ref[0, r0:r0 + c_len, 0:GLA_QK_W] * GLA_SCALE
    kc = gqk_ref[0, r0:r0 + c_len, GLA_QK_W:2 * GLA_QK_W]
    vc = gv_ref[0, r0:r0 + c_len, :]
    rel = cc - mid
    q_in = qc * jnp.exp(rel)
    k_in = (kc * jnp.exp(-rel)).astype(BF16)
    k_up = kc * jnp.exp(last - cc)
    q_st = qc * jnp.exp(cc)
    qbd_in = jnp.concatenate([q_in * m for m in head_mask], axis=0).astype(BF16)
    qbd_st = jnp.concatenate([q_st * m for m in head_mask], axis=0).astype(BF16)
    scores = _dot_nt(qbd_in, k_in)
    p = jnp.where(keep, scores, 0.0).astype(BF16)
    o_state = _dot(qbd_st, state.astype(BF16))
    xt = jnp.concatenate([k_up, jnp.broadcast_to(last, (c_len, GLA_QK_W))], axis=0).T
    k_up_t = xt[:, 0:c_len].astype(BF16)
    decay_col = jnp.exp(xt[:, c_len:c_len + 1])
    upd = []
    for h in range(GLA_HEADS):
        rows = slice(c_len * h, c_len * (h + 1))
        vh = vc[:, GLA_DV * h:GLA_DV * (h + 1)]
        out_ref[0, r0:r0 + c_len, GLA_DV * h:GLA_DV * (h + 1)] = _dot(p[rows], vh) + o_state[rows]
        upd.append(_dot(k_up_t[rows], vh))
    return decay_col * state + jnp.concatenate(upd, axis=0)


def _kernel_gla(gqkf_ref, gvf_ref, zf_ref, gqkb_ref, gvb_ref, zb_ref,
                w2f_ref, b2f_ref, w2b_ref, b2b_ref, lower_ref, upper_ref,
                of_ref, ob_ref, state_ref):
    @pl.when(pl.program_id(1) == 0)
    def _():
        state_ref[...] = jnp.zeros_like(state_ref)

    c_len = GLA_CHUNK
    n_chunks = gqkf_ref.shape[1] // c_len
    cum_f = _gla_log_decay_cumsum(zf_ref, w2f_ref, b2f_ref, lower_ref)
    cum_b = _gla_log_decay_cumsum(zb_ref, w2b_ref, b2b_ref, upper_ref)
    lane_head = lax.broadcasted_iota(jnp.int32, (c_len, GLA_QK_W), 1) // GLA_DK
    head_mask = [(lane_head == h).astype(F32) for h in range(GLA_HEADS)]
    row = lax.broadcasted_iota(jnp.int32, (GLA_HEADS * c_len, c_len), 0) % c_len
    col = lax.broadcasted_iota(jnp.int32, (GLA_HEADS * c_len, c_len), 1)
    state_f, state_b = state_ref[0], state_ref[1]
    for c in range(n_chunks):
        state_f = _gla_chunk(gqkf_ref, gvf_ref, of_ref, cum_f, state_f, c, head_mask, col <= row, False)
        state_b = _gla_chunk(gqkb_ref, gvb_ref, ob_ref, cum_b, state_b, n_chunks - 1 - c, head_mask,
                             col > row, True)
    state_ref[0] = state_f
    state_ref[1] = state_b


def _call_gla(gqk, gv, z, w):
    bsz, n, _ = gqk.shape
    tn = GLA_TILE
    nb = n // tn
    fwd = lambda width: pl.BlockSpec((1, tn, width), lambda b, j: (b, j, 0))
    bwd = lambda width: pl.BlockSpec((1, tn, width), lambda b, j: (b, nb - 1 - j, 0))
    o_shape = jax.ShapeDtypeStruct((bsz, n, GLA_V_W), F32)
    return pl.pallas_call(
        _kernel_gla,
        grid=(bsz, nb),
        in_specs=[fwd(2 * GLA_QK_W), fwd(GLA_V_W), fwd(Z_W), bwd(2 * GLA_QK_W), bwd(GLA_V_W), bwd(Z_W),
                  _const_spec((Z_W, GLA_QK_W)), _const_spec((1, GLA_QK_W)),
                  _const_spec((Z_W, GLA_QK_W)), _const_spec((1, GLA_QK_W)),
                  _const_spec((CUM_TILE, CUM_TILE)), _const_spec((CUM_TILE, CUM_TILE))],
        out_specs=(fwd(GLA_V_W), bwd(GLA_V_W)),
        out_shape=(o_shape, o_shape),
        scratch_shapes=[pltpu.VMEM((2, GLA_HEADS * GLA_DK, GLA_DV), F32)],
        compiler_params=pltpu.CompilerParams(dimension_semantics=("arbitrary", "arbitrary"),
                                             vmem_limit_bytes=VMEM_LIMIT),
        name="gla_bidir",
    )(gqk, gv, z, gqk, gv, z, w["w2f"], w["b2f"], w["w2b"], w["b2b"], w["tri_lower"], w["tri_upper"])


def _kernel_attn(flag_ref, q_ref, k_ref, vt_ref, o_ref, st_ref, m_ref):
    n = k_ref.shape[2]
    sub = ATTN_UNIT_ROWS
    units = [(s, pair) for s in range(q_ref.shape[1] // sub) for pair in range(ATTN_HEADS // ATTN_KV_HEADS // 2)]
    n_units = len(units)
    tiles = [slice(t, t + ATTN_KEY_TILE) for t in range(0, n, ATTN_KEY_TILE)]

    def run_phase(phase):
        u1, u2 = phase, phase - 1
        if u1 < n_units:
            s, pair = units[u1]
            heads = [q_ref[0, sub * s:sub * (s + 1), ATTN_DH * h:ATTN_DH * (h + 1)] for h in (2 * pair, 2 * pair + 1)]
            q_cur = jnp.concatenate(heads, axis=0)
            m_run = None
        if u2 >= 0:
            m_fin = m_ref[u2]
            acc = jnp.zeros((VT_ROWS, 2 * sub), F32)
        for tile in tiles:
            if u1 < n_units:
                st = _dot_nt(k_ref[0, 0, tile, :], q_cur)
                st_ref[u1 % 2, tile, :] = st
                m_tile = jnp.max(st, axis=0, keepdims=True)
                m_run = m_tile if m_run is None else jnp.maximum(m_run, m_tile)
            if u2 >= 0:
                pt = jnp.exp2(st_ref[u2 % 2, tile, :] - m_fin).astype(BF16)
                acc = acc + _dot(vt_ref[0, 0, :, tile], pt)
        if u1 < n_units:
            m_ref[u1] = m_run
        if u2 >= 0:
            s, pair = units[u2]
            o2 = (acc[0:ATTN_DH] / acc[ATTN_DH:ATTN_DH + 1]).T
            o_ref[0, sub * s:sub * (s + 1), LANES * pair:LANES * (pair + 1)] = (
                jnp.concatenate([o2[:sub], o2[sub:]], axis=1).astype(BF16))

    for phase in range(n_units + 1):
        pl.when(flag_ref[phase] == 0)(functools.partial(run_phase, phase))

def _call_attn(q, k, vt):
    bsz, n, _ = q.shape
    tq = ATTN_Q_TILE
    group_w = ATTN_Q_W // ATTN_KV_HEADS
    n_units = (tq // ATTN_UNIT_ROWS) * (group_w // LANES)
    q_spec = pl.BlockSpec((1, tq, group_w), lambda b, g, i: (b, i, g))
    k_spec = pl.BlockSpec((1, 1, n, ATTN_DH), lambda b, g, i: (b, g, 0, 0))
    vt_spec = pl.BlockSpec((1, 1, VT_ROWS, n), lambda b, g, i: (b, g, 0, 0))
    return pl.pallas_call(
        _kernel_attn,
        grid=(bsz, ATTN_KV_HEADS, n // tq),
        in_specs=[pl.BlockSpec(memory_space=pltpu.SMEM), q_spec, k_spec, vt_spec],
        out_specs=q_spec,
        out_shape=jax.ShapeDtypeStruct((bsz, n, ATTN_Q_W), BF16),
        scratch_shapes=[pltpu.VMEM((2, n, 2 * ATTN_UNIT_ROWS), F32),
                        pltpu.VMEM((n_units, 1, 2 * ATTN_UNIT_ROWS), F32)],
        compiler_params=pltpu.CompilerParams(
            dimension_semantics=("arbitrary", "arbitrary", "arbitrary"),
            vmem_limit_bytes=VMEM_LIMIT),
        name="gqa_attention",
    )(jnp.zeros((n_units + 1,), jnp.int32), q, k, vt)


def _kernel_b(h1_ref, of_ref, ob_ref, gr_ref, oa_ref, p_ref, gn_ref, wog_ref, woa_ref,
              ln2g_ref, ln2b_ref, wg_ref, wu_ref, wd_ref, ln3g_ref, ln3b_ref, wpg_ref, bpg_ref, wpe_ref,
              out_ref):
    o = of_ref[0] + ob_ref[0]
    gn = gn_ref[...]
    normed = []
    for h in range(GLA_HEADS):
        oh = o[:, GLA_DV * h:GLA_DV * (h + 1)]
        ms = jnp.mean(oh * oh, axis=-1, keepdims=True)
        normed.append(oh * lax.rsqrt(ms + GN_EPS) * gn)
    gr = gr_ref[0]
    o_gla = jnp.concatenate(normed, axis=1) * (gr * jax.nn.sigmoid(gr))
    mix = _dot(o_gla.astype(BF16), wog_ref[...]) + _dot(oa_ref[0], woa_ref[...])
    h2 = _layer_norm(DEEPNORM_ALPHA * h1_ref[0] + mix, ln2g_ref[...], ln2b_ref[...])
    f = _swiglu(h2.astype(BF16), wg_ref, wu_ref, wd_ref)
    h3 = _layer_norm(DEEPNORM_ALPHA * h2 + 0.5 * f, ln3g_ref[...], ln3b_ref[...])
    gate = jax.nn.sigmoid(_dot(h3.astype(BF16), wpg_ref[...]) + bpg_ref[...])
    out_ref[0] = h3 + gate * _dot(p_ref[0].astype(BF16), wpe_ref[...])


def _call_b(h1, o_f, o_b, gr, o_att, p, w):
    bsz, n, _ = h1.shape
    tm = TOKEN_TILE
    tok = lambda width: pl.BlockSpec((1, tm, width), lambda b, i: (b, i, 0))
    return pl.pallas_call(
        _kernel_b,
        grid=(bsz, n // tm),
        in_specs=[tok(D_MODEL), tok(GLA_V_W), tok(GLA_V_W), tok(GLA_V_W), tok(ATTN_Q_W), tok(P_DIM),
                  _const_spec((1, GLA_DV)),
                  _const_spec((GLA_V_W, D_MODEL)), _const_spec((ATTN_Q_W, D_MODEL)),
                  _const_spec((1, D_MODEL)), _const_spec((1, D_MODEL)),
                  _const_spec((D_MODEL, D_FF)), _const_spec((D_MODEL, D_FF)), _const_spec((D_FF, D_MODEL)),
                  _const_spec((1, D_MODEL)), _const_spec((1, D_MODEL)),
                  _const_spec((D_MODEL, D_MODEL)), _const_spec((1, D_MODEL)), _const_spec((P_DIM, D_MODEL))],
        out_specs=tok(D_MODEL),
        out_shape=jax.ShapeDtypeStruct((bsz, n, D_MODEL), F32),
        compiler_params=pltpu.CompilerParams(dimension_semantics=("arbitrary", "arbitrary"),
                                             vmem_limit_bytes=VMEM_LIMIT),
        name="outproj_ffn2_embed",
    )(h1, o_f, o_b, gr, o_att, p, w["gn_g"], w["w_out_gla"], w["w_out_att"], w["ln2_g"], w["ln2_b"],
      w["ffn2_wg"], w["ffn2_wu"], w["ffn2_wd"], w["ln3_g"], w["ln3_b"], w["w_pg"], w["b_pg"], w["w_pe"])


def _rope_tables(n):
    t = jnp.arange(n, dtype=jnp.int32)
    row = (t // GRID_W).astype(F32)
    col = (t % GRID_W).astype(F32)
    axis_dim = ATTN_DH // 2
    inv_freq = ROPE_THETA ** (-jnp.arange(0, axis_dim, 2, dtype=F32) / axis_dim)
    lane = jnp.arange(LANES, dtype=jnp.int32) % ATTN_DH
    freq = inv_freq[lane % ROPE_HALF]
    pos = jnp.where((lane // axis_dim)[None, :] == 0, row[:, None], col[:, None])
    ang = pos * freq[None, :]
    first_half = ((lane % axis_dim) // ROPE_HALF == 0)[None, :]
    cos, sin = jnp.cos(ang), jnp.sin(ang)
    return cos, jnp.where(first_half, -sin, 0.0), jnp.where(first_half, 0.0, sin)


def _prepare(ffn1_wg, ffn1_wu, ffn1_wd, ln1_g, ln1_b, w_in, gla_w2f, gla_b2f, gla_w2b, gla_b2b,
             gla_gn_g, q_norm_g, k_norm_g, w_out, ln2_g, ln2_b, ffn2_wg, ffn2_wu, ffn2_wd,
             ln3_g, ln3_b, w_pg, b_pg, w_pe):
    row = lambda v: v.reshape(1, -1).astype(F32)
    z0 = 2 * GLA_QK_W + 2 * GLA_V_W
    w_in_r = jnp.concatenate([w_in[:, :z0], w_in[:, z0 + Z_W:], w_in[:, z0:z0 + Z_W]], axis=1)
    zeros = jnp.zeros((GLA_GATE_RANK, GLA_QK_W), F32)
    idx = jnp.arange(ATTN_Q_W)
    avg = jnp.where((idx[:, None] // ATTN_DH) == (idx[None, :] // ATTN_DH), 1.0 / ATTN_DH, 0.0)
    ci = jnp.arange(CUM_TILE)
    same_chunk = (ci[:, None] // GLA_CHUNK) == (ci[None, :] // GLA_CHUNK)
    lower = jnp.where(same_chunk & (ci[None, :] <= ci[:, None]), 1.0, 0.0)
    return {
        "ffn1_wg": ffn1_wg.astype(BF16), "ffn1_wu": ffn1_wu.astype(BF16), "ffn1_wd": ffn1_wd.astype(BF16),
        "ln1_g": row(ln1_g), "ln1_b": row(ln1_b),
        "w_in": w_in_r.astype(BF16), "avg": avg.astype(BF16),
        "q_gain": row(jnp.tile(q_norm_g, ATTN_HEADS)), "k_gain": row(jnp.tile(k_norm_g, ATTN_KV_HEADS)),
        "w2f": jnp.concatenate([gla_w2f, zeros], axis=0).astype(BF16), "b2f": row(gla_b2f),
        "w2b": jnp.concatenate([zeros, gla_w2b], axis=0).astype(BF16), "b2b": row(gla_b2b),
        "tri_lower": lower.astype(BF16), "tri_upper": lower.T.astype(BF16),
        "gn_g": row(gla_gn_g),
        "w_out_gla": w_out[:GLA_V_W].astype(BF16), "w_out_att": w_out[GLA_V_W:].astype(BF16),
        "ln2_g": row(ln2_g), "ln2_b": row(ln2_b),
        "ffn2_wg": ffn2_wg.astype(BF16), "ffn2_wu": ffn2_wu.astype(BF16), "ffn2_wd": ffn2_wd.astype(BF16),
        "ln3_g": row(ln3_g), "ln3_b": row(ln3_b),
        "w_pg": w_pg.astype(BF16), "b_pg": row(b_pg), "w_pe": w_pe.astype(BF16),
    }


def _encoder_layer(x, p, w):
    n = x.shape[1]
    assert n % GLA_TILE == 0 and n % ATTN_Q_TILE == 0 and n % TOKEN_TILE == 0 and n % GRID_W == 0
    h1, gqk, gv, gr, z, q, k, v = _call_a(x, w, _rope_tables(n))
    o_f, o_b = _call_gla(gqk, gv, z, w)
    o_att = _call_attn(q, k, v)
    return _call_b(h1, o_f, o_b, gr, o_att, p, w)


def kernel(x_prompt, x_sample, p_prompt, p_sample, ffn1_wg, ffn1_wu, ffn1_wd, ln1_g, ln1_b, w_in,
           gla_w2f, gla_b2f, gla_w2b, gla_b2b, gla_gn_g, q_norm_g, k_norm_g, w_out, ln2_g, ln2_b,
           ffn2_wg, ffn2_wu, ffn2_wd, ln3_g, ln3_b, w_pg, b_pg, w_pe):
    y_prompt, y_sample = x_prompt, x_sample
    for i in range(DEPTH):
        w = _prepare(ffn1_wg[i], ffn1_wu[i], ffn1_wd[i], ln1_g[i], ln1_b[i], w_in[i],
                     gla_w2f[i], gla_b2f[i], gla_w2b[i], gla_b2b[i], gla_gn_g[i], q_norm_g[i],
                     k_norm_g[i], w_out[i], ln2_g[i], ln2_b[i], ffn2_wg[i], ffn2_wu[i], ffn2_wd[i],
                     ln3_g[i], ln3_b[i], w_pg[i], b_pg[i], w_pe[i])
        y_prompt = _encoder_layer(y_prompt, p_prompt[i], w)
        y_sample = _encoder_layer(y_sample, p_sample[i], w)
    return (y_prompt, y_sample)
```

```python
import functools

import jax
import jax.numpy as jnp
from jax import lax
from jax.experimental import pallas as pl
from jax.experimental.pallas import tpu as pltpu

F32 = jnp.float32
BF16 = jnp.bfloat16

D_MODEL = 1024
D_FF = 2816
P_DIM = 256
GRID_W = 64
DEPTH = 1
GLA_HEADS = 4
GLA_DK = 64
GLA_DV = 128
GLA_GATE_RANK = 16
GLA_TAU = 16.0
GLA_CHUNK = 64
ATTN_HEADS = 8
ATTN_KV_HEADS = 2
ATTN_DH = 64
ROPE_THETA = 10000.0
LN_EPS = 1e-5
QK_EPS = 1e-6
GN_EPS = 1e-5
DEEPNORM_ALPHA = (2.0 * DEPTH) ** 0.25

GLA_QK_W = GLA_HEADS * GLA_DK
GLA_V_W = GLA_HEADS * GLA_DV
ATTN_Q_W = ATTN_HEADS * ATTN_DH
ATTN_KV_W = ATTN_KV_HEADS * ATTN_DH
Z_W = 2 * GLA_GATE_RANK
OFF_GQK = 0
OFF_GV = OFF_GQK + 2 * GLA_QK_W
OFF_GR = OFF_GV + GLA_V_W
OFF_AQ = OFF_GR + GLA_V_W
OFF_AK = OFF_AQ + ATTN_Q_W
OFF_AV = OFF_AK + ATTN_KV_W
OFF_Z = OFF_AV + ATTN_KV_W
D_IN = OFF_Z + Z_W

LANES = 128
ROPE_HALF = ATTN_DH // 4
QK_SCALE = ATTN_DH ** -0.5 * 1.4426950408889634
GLA_SCALE = GLA_DK ** -0.5

TOKEN_TILE = 512
FF_CHUNKS = ((0, 1536), (1536, D_FF))
GLA_TILE = 1024
CUM_TILE = 256
ATTN_Q_TILE = 1024
ATTN_UNIT_ROWS = 256
ATTN_UNIT_HEADS = 2
ATTN_KEY_TILE = 512
VT_ROWS = ATTN_DH + 16
VMEM_LIMIT = 56 * 1024 * 1024


def _const_spec(shape):
    return pl.BlockSpec(shape, lambda *_: (0,) * len(shape), pipeline_mode=pl.Buffered(1))


def _dot(a, b):
    return jnp.dot(a, b, preferred_element_type=F32)


def _dot_nt(a, b):
    return lax.dot_general(a, b, (((1,), (1,)), ((), ())), preferred_element_type=F32)


def _split_bf16(x):
    hi = x.astype(BF16)
    lo = (x - hi.astype(F32)).astype(BF16)
    return hi, lo


def _layer_norm(y, g, b):
    mu = jnp.mean(y, axis=-1, keepdims=True)
    d = y - mu
    var = jnp.mean(d * d, axis=-1, keepdims=True)
    return d * lax.rsqrt(var + LN_EPS) * g + b


def _swiglu(xb, wg_ref, wu_ref, wd_ref):
    out = None
    for lo, hi in FF_CHUNKS:
        g = _dot(xb, wg_ref[:, lo:hi])
        u = _dot(xb, wu_ref[:, lo:hi])
        hid = (g * jax.nn.sigmoid(g)) * u
        part = _dot(hid.astype(BF16), wd_ref[lo:hi, :])
        out = part if out is None else out + part
    return out


def _head_rms(x, avg, gain, eps):
    hi, lo = _split_bf16(x * x)
    ms = _dot(hi, avg) + _dot(lo, avg)
    return x * lax.rsqrt(ms + eps) * gain


def _rope(xs, c, sa, sb):
    return (xs * c + pltpu.roll(xs, LANES - ROPE_HALF, 1) * sa
            + pltpu.roll(xs, ROPE_HALF, 1) * sb)


def _kernel_a(x_ref, wg_ref, wu_ref, wd_ref, lng_ref, lnb_ref, win_ref, avg_ref, qg_ref, kg_ref,
              c_ref, sa_ref, sb_ref,
              h_ref, gqk_ref, gv_ref, gr_ref, z_ref, q_ref, k_ref, v_ref):
    x = x_ref[0]
    f = _swiglu(x.astype(BF16), wg_ref, wu_ref, wd_ref)
    h = _layer_norm(DEEPNORM_ALPHA * x + 0.5 * f, lng_ref[...], lnb_ref[...])
    h_ref[0] = h
    proj = _dot(h.astype(BF16), win_ref[...])
    gqk_ref[0] = proj[:, OFF_GQK:OFF_GV]
    gv_ref[0] = proj[:, OFF_GV:OFF_GR].astype(BF16)
    gr_ref[0] = proj[:, OFF_GR:OFF_AQ]
    z_ref[0] = proj[:, OFF_Z:D_IN]
    c, sa, sb = c_ref[...], sa_ref[...], sb_ref[...]
    avg = avg_ref[...]
    qn = _head_rms(proj[:, OFF_AQ:OFF_AK], avg, qg_ref[...], QK_EPS)
    for j in range(ATTN_Q_W // LANES):
        sl = slice(LANES * j, LANES * (j + 1))
        q_ref[0, :, sl] = (_rope(qn[:, sl], c, sa, sb) * QK_SCALE).astype(BF16)
    kn = _head_rms(proj[:, OFF_AK:OFF_AV], avg[:ATTN_KV_W, :ATTN_KV_W], kg_ref[...], QK_EPS)
    kr = _rope(kn, c, sa, sb).astype(BF16)
    vt = proj[:, OFF_AV:OFF_Z].T.astype(BF16)
    ones = jnp.ones((VT_ROWS - ATTN_DH, vt.shape[1]), BF16)
    for g in range(ATTN_KV_HEADS):
        k_ref[0, g] = kr[:, ATTN_DH * g:ATTN_DH * (g + 1)]
        v_ref[0, g, 0:ATTN_DH, :] = vt[ATTN_DH * g:ATTN_DH * (g + 1), :]
        v_ref[0, g, ATTN_DH:VT_ROWS, :] = ones


def _call_a(x, w, tabs):
    bsz, n, _ = x.shape
    tm = TOKEN_TILE
    tok = lambda width: pl.BlockSpec((1, tm, width), lambda b, i: (b, i, 0))
    tab = pl.BlockSpec((tm, LANES), lambda b, i: (i, 0))
    kv_spec = pl.BlockSpec((1, ATTN_KV_HEADS, tm, ATTN_DH), lambda b, i: (b, 0, i, 0))
    vt_spec = pl.BlockSpec((1, ATTN_KV_HEADS, VT_ROWS, tm), lambda b, i: (b, 0, 0, i))
    out_shape = (
        jax.ShapeDtypeStruct((bsz, n, D_MODEL), F32),
        jax.ShapeDtypeStruct((bsz, n, 2 * GLA_QK_W), F32),
        jax.ShapeDtypeStruct((bsz, n, GLA_V_W), BF16),
        jax.ShapeDtypeStruct((bsz, n, GLA_V_W), F32),
        jax.ShapeDtypeStruct((bsz, n, Z_W), F32),
        jax.ShapeDtypeStruct((bsz, n, ATTN_Q_W), BF16),
        jax.ShapeDtypeStruct((bsz, ATTN_KV_HEADS, n, ATTN_DH), BF16),
        jax.ShapeDtypeStruct((bsz, ATTN_KV_HEADS, VT_ROWS, n), BF16),
    )
    return pl.pallas_call(
        _kernel_a,
        grid=(bsz, n // tm),
        in_specs=[tok(D_MODEL),
                  _const_spec((D_MODEL, D_FF)), _const_spec((D_MODEL, D_FF)), _const_spec((D_FF, D_MODEL)),
                  _const_spec((1, D_MODEL)), _const_spec((1, D_MODEL)),
                  _const_spec((D_MODEL, D_IN)), _const_spec((ATTN_Q_W, ATTN_Q_W)),
                  _const_spec((1, ATTN_Q_W)), _const_spec((1, ATTN_KV_W)),
                  tab, tab, tab],
        out_specs=(tok(D_MODEL), tok(2 * GLA_QK_W), tok(GLA_V_W), tok(GLA_V_W), tok(Z_W), tok(ATTN_Q_W),
                   kv_spec, vt_spec),
        out_shape=out_shape,
        compiler_params=pltpu.CompilerParams(dimension_semantics=("arbitrary", "arbitrary"),
                                             vmem_limit_bytes=VMEM_LIMIT),
        name="ffn1_inproj",
    )(x, w["ffn1_wg"], w["ffn1_wu"], w["ffn1_wd"], w["ln1_g"], w["ln1_b"], w["w_in"], w["avg"],
      w["q_gain"], w["k_gain"], *tabs)


def _gla_log_decay_cumsum(z_ref, w2_ref, b2_ref, tri_ref):
    tn = z_ref.shape[1]
    pre = _dot(z_ref[0].astype(BF16), w2_ref[...]) + b2_ref[...]
    log_a = (jnp.minimum(pre, 0.0) - jnp.log(1.0 + jnp.exp(-jnp.abs(pre)))) * (1.0 / GLA_TAU)
    hi, lo = _split_bf16(log_a)
    tri = tri_ref[...]
    return jnp.concatenate(
        [_dot(tri, hi[r:r + CUM_TILE]) + _dot(tri, lo[r:r + CUM_TILE]) for r in range(0, tn, CUM_TILE)],
        axis=0)


def _gla_chunk(gqk_ref, gv_ref, out_ref, cum, state, c, head_mask, keep, reverse):
    c_len = GLA_CHUNK
    r0 = c_len * c
    cc = cum[r0:r0 + c_len]
    mid = cc[c_len // 2:c_len // 2 + 1] if reverse else cc[c_len // 2 - 1:c_len // 2]
    last = cc[0:1] if reverse else cc[c_len - 1:c_len]
    qc = gqk_ref[0, r0:r0 + c_len, 0:GLA_QK_W] * GLA_SCALE
    kc = gqk_ref[0, r0:r0 + c_len, GLA_QK_W:2 * GLA_QK_W]
    vc = gv_ref[0, r0:r0 + c_len, :]
    rel = cc - mid
    q_in = qc * jnp.exp(rel)
    k_in = (kc * jnp.exp(-rel)).astype(BF16)
    k_up = kc * jnp.exp(last - cc)
    q_st = qc * jnp.exp(cc)
    qbd_in = jnp.concatenate([q_in * m for m in head_mask], axis=0).astype(BF16)
    qbd_st = jnp.concatenate([q_st * m for m in head_mask], axis=0).astype(BF16)
    scores = _dot_nt(qbd_in, k_in)
    p = jnp.where(keep, scores, 0.0).astype(BF16)
    o_state = _dot(qbd_st, state.astype(BF16))
    xt = jnp.concatenate([k_up, jnp.broadcast_to(last, (c_len, GLA_QK_W))], axis=0).T
    k_up_t = xt[:, 0:c_len].astype(BF16)
    decay_col = jnp.exp(xt[:, c_len:c_len + 1])
    upd = []
    for h in range(GLA_HEADS):
        rows = slice(c_len * h, c_len * (h + 1))
        vh = vc[:, GLA_DV * h:GLA_DV * (h + 1)]
        out_ref[0, r0:r0 + c_len, GLA_DV * h:GLA_DV * (h + 1)] = _dot(p[rows], vh) + o_state[rows]
        upd.append(_dot(k_up_t[rows], vh))
    return decay_col * state + jnp.concatenate(upd, axis=0)


def _kernel_gla(gqkf_ref, gvf_ref, zf_ref, gqkb_ref, gvb_ref, zb_ref,
                w2f_ref, b2f_ref, w2b_ref, b2b_ref, lower_ref, upper_ref,
                of_ref, ob_ref, state_ref):
    @pl.when(pl.program_id(1) == 0)
    def _():
        state_ref[...] = jnp.zeros_like(state_ref)

    c_len = GLA_CHUNK
    n_chunks = gqkf_ref.shape[1] // c_len
    cum_f = _gla_log_decay_cumsum(zf_ref, w2f_ref, b2f_ref, lower_ref)
    cum_b = _gla_log_decay_cumsum(zb_ref, w2b_ref, b2b_ref, upper_ref)
    lane_head = lax.broadcasted_iota(jnp.int32, (c_len, GLA_QK_W), 1) // GLA_DK
    head_mask = [(lane_head == h).astype(F32) for h in range(GLA_HEADS)]
    row = lax.broadcasted_iota(jnp.int32, (GLA_HEADS * c_len, c_len), 0) % c_len
    col = lax.broadcasted_iota(jnp.int32, (GLA_HEADS * c_len, c_len), 1)
    state_f, state_b = state_ref[0], state_ref[1]
    for c in range(n_chunks):
        state_f = _gla_chunk(gqkf_ref, gvf_ref, of_ref, cum_f, state_f, c, head_mask, col <= row, False)
        state_b = _gla_chunk(gqkb_ref, gvb_ref, ob_ref, cum_b, state_b, n_chunks - 1 - c, head_mask,
                             col > row, True)
    state_ref[0] = state_f
    state_ref[1] = state_b


def _call_gla(gqk, gv, z, w):
    bsz, n, _ = gqk.shape
    tn = GLA_TILE
    nb = n // tn
    fwd = lambda width: pl.BlockSpec((1, tn, width), lambda b, j: (b, j, 0))
    bwd = lambda width: pl.BlockSpec((1, tn, width), lambda b, j: (b, nb - 1 - j, 0))
    o_shape = jax.ShapeDtypeStruct((bsz, n, GLA_V_W), F32)
    return pl.pallas_call(
        _kernel_gla,
        grid=(bsz, nb),
        in_specs=[fwd(2 * GLA_QK_W), fwd(GLA_V_W), fwd(Z_W), bwd(2 * GLA_QK_W), bwd(GLA_V_W), bwd(Z_W),
                  _const_spec((Z_W, GLA_QK_W)), _const_spec((1, GLA_QK_W)),
                  _const_spec((Z_W, GLA_QK_W)), _const_spec((1, GLA_QK_W)),
                  _const_spec((CUM_TILE, CUM_TILE)), _const_spec((CUM_TILE, CUM_TILE))],
        out_specs=(fwd(GLA_V_W), bwd(GLA_V_W)),
        out_shape=(o_shape, o_shape),
        scratch_shapes=[pltpu.VMEM((2, GLA_HEADS * GLA_DK, GLA_DV), F32)],
        compiler_params=pltpu.CompilerParams(dimension_semantics=("arbitrary", "arbitrary"),
                                             vmem_limit_bytes=VMEM_LIMIT),
        name="gla_bidir",
    )(gqk, gv, z, gqk, gv, z, w["w2f"], w["b2f"], w["w2b"], w["b2b"], w["tri_lower"], w["tri_upper"])


def _kernel_attn(flag_ref, q_ref, k_ref, vt_ref, o_ref, st_ref, m_ref):
    n = k_ref.shape[2]
    sub = ATTN_UNIT_ROWS
    hpu = ATTN_UNIT_HEADS
    width = hpu * sub
    units = [(s, hg) for s in range(q_ref.shape[1] // sub) for hg in range(ATTN_HEADS // ATTN_KV_HEADS // hpu)]
    n_units = len(units)
    tiles = [slice(t, t + ATTN_KEY_TILE) for t in range(0, n, ATTN_KEY_TILE)]

    def run_phase(phase):
        u1, u2 = phase, phase - 1
        if u1 < n_units:
            s, hg = units[u1]
            heads = [q_ref[0, sub * s:sub * (s + 1), ATTN_DH * h:ATTN_DH * (h + 1)]
                     for h in range(hpu * hg, hpu * (hg + 1))]
            q_cur = jnp.concatenate(heads, axis=0)
            m_run = None
        if u2 >= 0:
            m_fin = m_ref[u2]
            acc = jnp.zeros((VT_ROWS, width), F32)
        for tile in tiles:
            if u2 >= 0:
                pt = jnp.exp2(st_ref[u2 % 2, tile, :] - m_fin).astype(BF16)
                acc = acc + _dot(vt_ref[0, 0, :, tile], pt)
            if u1 < n_units:
                st = _dot_nt(k_ref[0, 0, tile, :], q_cur)
                st_ref[u1 % 2, tile, :] = st
                m_tile = jnp.max(st, axis=0, keepdims=True)
                m_run = m_tile if m_run is None else jnp.maximum(m_run, m_tile)
        if u1 < n_units:
            m_ref[u1] = m_run
        if u2 >= 0:
            s, hg = units[u2]
            o2 = (acc[0:ATTN_DH] / acc[ATTN_DH:ATTN_DH + 1]).T
            o_ref[0, sub * s:sub * (s + 1), ATTN_DH * hpu * hg:ATTN_DH * hpu * (hg + 1)] = (
                jnp.concatenate([o2[sub * i:sub * (i + 1)] for i in range(hpu)], axis=1).astype(BF16))

    for phase in range(n_units + 1):
        pl.when(flag_ref[phase] == 0)(functools.partial(run_phase, phase))

def _call_attn(q, k, vt):
    bsz, n, _ = q.shape
    tq = ATTN_Q_TILE
    group_w = ATTN_Q_W // ATTN_KV_HEADS
    width = ATTN_UNIT_HEADS * ATTN_UNIT_ROWS
    n_units = (tq // ATTN_UNIT_ROWS) * (group_w // (ATTN_UNIT_HEADS * ATTN_DH))
    q_spec = pl.BlockSpec((1, tq, group_w), lambda b, g, i: (b, i, g))
    k_spec = pl.BlockSpec((1, 1, n, ATTN_DH), lambda b, g, i: (b, g, 0, 0))
    vt_spec = pl.BlockSpec((1, 1, VT_ROWS, n), lambda b, g, i: (b, g, 0, 0))
    return pl.pallas_call(
        _kernel_attn,
        grid=(bsz, ATTN_KV_HEADS, n // tq),
        in_specs=[pl.BlockSpec(memory_space=pltpu.SMEM), q_spec, k_spec, vt_spec],
        out_specs=q_spec,
        out_shape=jax.ShapeDtypeStruct((bsz, n, ATTN_Q_W), BF16),
        scratch_shapes=[pltpu.VMEM((2, n, width), F32),
                        pltpu.VMEM((n_units, 1, width), F32)],
        compiler_params=pltpu.CompilerParams(
            dimension_semantics=("arbitrary", "arbitrary", "arbitrary"),
            vmem_limit_bytes=VMEM_LIMIT),
        name="gqa_attention",
    )(jnp.zeros((n_units + 1,), jnp.int32), q, k, vt)


def _kernel_b(h1_ref, of_ref, ob_ref, gr_ref, oa_ref, p_ref, gn_ref, wog_ref, woa_ref,
              ln2g_ref, ln2b_ref, wg_ref, wu_ref, wd_ref, ln3g_ref, ln3b_ref, wpg_ref, bpg_ref, wpe_ref,
              out_ref):
    o = of_ref[0] + ob_ref[0]
    gn = gn_ref[...]
    normed = []
    for h in range(GLA_HEADS):
        oh = o[:, GLA_DV * h:GLA_DV * (h + 1)]
        ms = jnp.mean(oh * oh, axis=-1, keepdims=True)
        normed.append(oh * lax.rsqrt(ms + GN_EPS) * gn)
    gr = gr_ref[0]
    o_gla = jnp.concatenate(normed, axis=1) * (gr * jax.nn.sigmoid(gr))
    mix = _dot(o_gla.astype(BF16), wog_ref[...]) + _dot(oa_ref[0], woa_ref[...])
    h2 = _layer_norm(DEEPNORM_ALPHA * h1_ref[0] + mix, ln2g_ref[...], ln2b_ref[...])
    f = _swiglu(h2.astype(BF16), wg_ref, wu_ref, wd_ref)
    h3 = _layer_norm(DEEPNORM_ALPHA * h2 + 0.5 * f, ln3g_ref[...], ln3b_ref[...])
    gate = jax.nn.sigmoid(_dot(h3.astype(BF16), wpg_ref[...]) + bpg_ref[...])
    out_ref[0] = h3 + gate * _dot(p_ref[0].astype(BF16), wpe_ref[...])


def _call_b(h1, o_f, o_b, gr, o_att, p, w):
    bsz, n, _ = h1.shape
    tm = TOKEN_TILE
    tok = lambda width: pl.BlockSpec((1, tm, width), lambda b, i: (b, i, 0))
    return pl.pallas_call(
        _kernel_b,
        grid=(bsz, n // tm),
        in_specs=[tok(D_MODEL), tok(GLA_V_W), tok(GLA_V_W), tok(GLA_V_W), tok(ATTN_Q_W), tok(P_DIM),
                  _const_spec((1, GLA_DV)),
                  _const_spec((GLA_V_W, D_MODEL)), _const_spec((ATTN_Q_W, D_MODEL)),
                  _const_spec((1, D_MODEL)), _const_spec((1, D_MODEL)),
                  _const_spec((D_MODEL, D_FF)), _const_spec((D_MODEL, D_FF)), _const_spec((D_FF, D_MODEL)),
                  _const_spec((1, D_MODEL)), _const_spec((1, D_MODEL)),
                  _const_spec((D_MODEL, D_MODEL)), _const_spec((1, D_MODEL)), _const_spec((P_DIM, D_MODEL))],
        out_specs=tok(D_MODEL),
        out_shape=jax.ShapeDtypeStruct((bsz, n, D_MODEL), F32),
        compiler_params=pltpu.CompilerParams(dimension_semantics=("arbitrary", "arbitrary"),
                                             vmem_limit_bytes=VMEM_LIMIT),
        name="outproj_ffn2_embed",
    )(h1, o_f, o_b, gr, o_att, p, w["gn_g"], w["w_out_gla"], w["w_out_att"], w["ln2_g"], w["ln2_b"],
      w["ffn2_wg"], w["ffn2_wu"], w["ffn2_wd"], w["ln3_g"], w["ln3_b"], w["w_pg"], w["b_pg"], w["w_pe"])


def _rope_tables(n):
    t = jnp.arange(n, dtype=jnp.int32)
    row = (t // GRID_W).astype(F32)
    col = (t % GRID_W).astype(F32)
    axis_dim = ATTN_DH // 2
    inv_freq = ROPE_THETA ** (-jnp.arange(0, axis_dim, 2, dtype=F32) / axis_dim)
    lane = jnp.arange(LANES, dtype=jnp.int32) % ATTN_DH
    freq = inv_freq[lane % ROPE_HALF]
    pos = jnp.where((lane // axis_dim)[None, :] == 0, row[:, None], col[:, None])
    ang = pos * freq[None, :]
    first_half = ((lane % axis_dim) // ROPE_HALF == 0)[None, :]
    cos, sin = jnp.cos(ang), jnp.sin(ang)
    return cos, jnp.where(first_half, -sin, 0.0), jnp.where(first_half, 0.0, sin)


def _prepare(ffn1_wg, ffn1_wu, ffn1_wd, ln1_g, ln1_b, w_in, gla_w2f, gla_b2f, gla_w2b, gla_b2b,
             gla_gn_g, q_norm_g, k_norm_g, w_out, ln2_g, ln2_b, ffn2_wg, ffn2_wu, ffn2_wd,
             ln3_g, ln3_b, w_pg, b_pg, w_pe):
    row = lambda v: v.reshape(1, -1).astype(F32)
    z0 = 2 * GLA_QK_W + 2 * GLA_V_W
    w_in_r = jnp.concatenate([w_in[:, :z0], w_in[:, z0 + Z_W:], w_in[:, z0:z0 + Z_W]], axis=1)
    zeros = jnp.zeros((GLA_GATE_RANK, GLA_QK_W), F32)
    idx = jnp.arange(ATTN_Q_W)
    avg = jnp.where((idx[:, None] // ATTN_DH) == (idx[None, :] // ATTN_DH), 1.0 / ATTN_DH, 0.0)
    ci = jnp.arange(CUM_TILE)
    same_chunk = (ci[:, None] // GLA_CHUNK) == (ci[None, :] // GLA_CHUNK)
    lower = jnp.where(same_chunk & (ci[None, :] <= ci[:, None]), 1.0, 0.0)
    return {
        "ffn1_wg": ffn1_wg.astype(BF16), "ffn1_wu": ffn1_wu.astype(BF16), "ffn1_wd": ffn1_wd.astype(BF16),
        "ln1_g": row(ln1_g), "ln1_b": row(ln1_b),
        "w_in": w_in_r.astype(BF16), "avg": avg.astype(BF16),
        "q_gain": row(jnp.tile(q_norm_g, ATTN_HEADS)), "k_gain": row(jnp.tile(k_norm_g, ATTN_KV_HEADS)),
        "w2f": jnp.concatenate([gla_w2f, zeros], axis=0).astype(BF16), "b2f": row(gla_b2f),
        "w2b": jnp.concatenate([zeros, gla_w2b], axis=0).astype(BF16), "b2b": row(gla_b2b),
        "tri_lower": lower.astype(BF16), "tri_upper": lower.T.astype(BF16),
        "gn_g": row(gla_gn_g),
        "w_out_gla": w_out[:GLA_V_W].astype(BF16), "w_out_att": w_out[GLA_V_W:].astype(BF16),
        "ln2_g": row(ln2_g), "ln2_b": row(ln2_b),
        "ffn2_wg": ffn2_wg.astype(BF16), "ffn2_wu": ffn2_wu.astype(BF16), "ffn2_wd": ffn2_wd.astype(BF16),
        "ln3_g": row(ln3_g), "ln3_b": row(ln3_b),
        "w_pg": w_pg.astype(BF16), "b_pg": row(b_pg), "w_pe": w_pe.astype(BF16),
    }


def _encoder_layer(x, p, w):
    n = x.shape[1]
    assert n % GLA_TILE == 0 and n % ATTN_Q_TILE == 0 and n % TOKEN_TILE == 0 and n % GRID_W == 0
    h1, gqk, gv, gr, z, q, k, v = _call_a(x, w, _rope_tables(n))
    o_f, o_b = _call_gla(gqk, gv, z, w)
    o_att = _call_attn(q, k, v)
    return _call_b(h1, o_f, o_b, gr, o_att, p, w)


def kernel(x_prompt, x_sample, p_prompt, p_sample, ffn1_wg, ffn1_wu, ffn1_wd, ln1_g, ln1_b, w_in,
           gla_w2f, gla_b2f, gla_w2b, gla_b2b, gla_gn_g, q_norm_g, k_norm_g, w_out, ln2_g, ln2_b,
           ffn2_wg, ffn2_wu, ffn2_wd, ln3_g, ln3_b, w_pg, b_pg, w_pe):
    y_prompt, y_sample = x_prompt, x_sample
    for i in range(DEPTH):
        w = _prepare(ffn1_wg[i], ffn1_wu[i], ffn1_wd[i], ln1_g[i], ln1_b[i], w_in[i],
                     gla_w2f[i], gla_b2f[i], gla_w2b[i], gla_b2b[i], gla_gn_g[i], q_norm_g[i],
                     k_norm_g[i], w_out[i], ln2_g[i], ln2_b[i], ffn2_wg[i], ffn2_wu[i], ffn2_wd[i],
                     ln3_g[i], ln3_b[i], w_pg[i], b_pg[i], w_pe[i])
        y_prompt = _encoder_layer(y_prompt, p_prompt[i], w)
        y_sample = _encoder_layer(y_sample, p_sample[i], w)
    return (y_prompt, y_sample)
```

```python
import functools

import jax
import jax.numpy as jnp
from jax import lax
from jax.experimental import pallas as pl
from jax.experimental.pallas import tpu as pltpu

F32 = jnp.float32
BF16 = jnp.bfloat16

D_MODEL = 1024
D_FF = 2816
P_DIM = 256
GRID_W = 64
DEPTH = 1
GLA_HEADS = 4
GLA_DK = 64
GLA_DV = 128
GLA_GATE_RANK = 16
GLA_TAU = 16.0
GLA_CHUNK = 64
ATTN_HEADS = 8
ATTN_KV_HEADS = 2
ATTN_DH = 64
ROPE_THETA = 10000.0
LN_EPS = 1e-5
QK_EPS = 1e-6
GN_EPS = 1e-5
DEEPNORM_ALPHA = (2.0 * DEPTH) ** 0.25

GLA_QK_W = GLA_HEADS * GLA_DK
GLA_V_W = GLA_HEADS * GLA_DV
ATTN_Q_W = ATTN_HEADS * ATTN_DH
ATTN_KV_W = ATTN_KV_HEADS * ATTN_DH
Z_W = 2 * GLA_GATE_RANK
OFF_GQK = 0
OFF_GV = OFF_GQK + 2 * GLA_QK_W
OFF_GR = OFF_GV + GLA_V_W
OFF_AQ = OFF_GR + GLA_V_W
OFF_AK = OFF_AQ + ATTN_Q_W
OFF_AV = OFF_AK + ATTN_KV_W
OFF_Z = OFF_AV + ATTN_KV_W
D_IN = OFF_Z + Z_W

LANES = 128
AVG_W = 256
ROPE_HALF = ATTN_DH // 4
LOG2E = 1.4426950408889634
QK_SCALE = ATTN_DH ** -0.5 * LOG2E
GLA_SCALE = GLA_DK ** -0.5

TOKEN_TILE = 512
FF_CHUNKS = ((0, 1536), (1536, D_FF))
GLA_TILE = 1024
CUM_TILE = 256
ATTN_Q_TILE = 1024
ATTN_UNIT_ROWS = 256
ATTN_UNIT_HEADS = 2
ATTN_KEY_TILE = 512
VT_ROWS = ATTN_DH + 16
VMEM_LIMIT = 56 * 1024 * 1024


def _const_spec(shape):
    return pl.BlockSpec(shape, lambda *_: (0,) * len(shape), pipeline_mode=pl.Buffered(1))


def _dot(a, b):
    return jnp.dot(a, b, preferred_element_type=F32)


def _dot_nt(a, b):
    return lax.dot_general(a, b, (((1,), (1,)), ((), ())), preferred_element_type=F32)


def _split_bf16(x):
    hi = x.astype(BF16)
    lo = (x - hi.astype(F32)).astype(BF16)
    return hi, lo


def _layer_norm(y, g, b):
    mu = jnp.mean(y, axis=-1, keepdims=True)
    d = y - mu
    var = jnp.mean(d * d, axis=-1, keepdims=True)
    return d * lax.rsqrt(var + LN_EPS) * g + b


def _swiglu(xb, wg_ref, wu_ref, wd_ref):
    out = None
    for lo, hi in FF_CHUNKS:
        g = _dot(xb, wg_ref[:, lo:hi])
        u = _dot(xb, wu_ref[:, lo:hi])
        hid = (g * jax.nn.sigmoid(g)) * u
        part = _dot(hid.astype(BF16), wd_ref[lo:hi, :])
        out = part if out is None else out + part
    return out


def _head_rms(x, avg, gain, eps):
    hi, lo = _split_bf16(x * x)
    w = min(x.shape[1], AVG_W)
    a = avg[:w, :w]
    ms = jnp.concatenate([_dot(hi[:, c:c + w], a) + _dot(lo[:, c:c + w], a) for c in range(0, x.shape[1], w)],
                         axis=1)
    return x * lax.rsqrt(ms + eps) * gain


def _rope(xs, c, sa, sb):
    return (xs * c + pltpu.roll(xs, LANES - ROPE_HALF, 1) * sa
            + pltpu.roll(xs, ROPE_HALF, 1) * sb)


def _kernel_a(x_ref, wg_ref, wu_ref, wd_ref, lng_ref, lnb_ref, win_ref, avg_ref, qg_ref, kg_ref,
              c_ref, sa_ref, sb_ref,
              h_ref, gqk_ref, gv_ref, gr_ref, z_ref, q_ref, k_ref, v_ref):
    x = x_ref[0]
    f = _swiglu(x.astype(BF16), wg_ref, wu_ref, wd_ref)
    h = _layer_norm(DEEPNORM_ALPHA * x + 0.5 * f, lng_ref[...], lnb_ref[...])
    h_ref[0] = h
    proj = _dot(h.astype(BF16), win_ref[...])
    gqk_ref[0] = proj[:, OFF_GQK:OFF_GV]
    gv_ref[0] = proj[:, OFF_GV:OFF_GR].astype(BF16)
    gr_ref[0] = proj[:, OFF_GR:OFF_AQ]
    z_ref[0] = proj[:, OFF_Z:D_IN]
    c, sa, sb = c_ref[...], sa_ref[...], sb_ref[...]
    avg = avg_ref[...]
    qn = _head_rms(proj[:, OFF_AQ:OFF_AK], avg, qg_ref[...], QK_EPS)
    for j in range(ATTN_Q_W // LANES):
        sl = slice(LANES * j, LANES * (j + 1))
        q_ref[0, :, sl] = (_rope(qn[:, sl], c, sa, sb) * QK_SCALE).astype(BF16)
    kn = _head_rms(proj[:, OFF_AK:OFF_AV], avg, kg_ref[...], QK_EPS)
    kr = _rope(kn, c, sa, sb).astype(BF16)
    vt = proj[:, OFF_AV:OFF_Z].T.astype(BF16)
    ones = jnp.ones((VT_ROWS - ATTN_DH, vt.shape[1]), BF16)
    for g in range(ATTN_KV_HEADS):
        k_ref[0, g] = kr[:, ATTN_DH * g:ATTN_DH * (g + 1)]
        v_ref[0, g, 0:ATTN_DH, :] = vt[ATTN_DH * g:ATTN_DH * (g + 1), :]
        v_ref[0, g, ATTN_DH:VT_ROWS, :] = ones


def _call_a(x, w, tabs):
    bsz, n, _ = x.shape
    tm = TOKEN_TILE
    tok = lambda width: pl.BlockSpec((1, tm, width), lambda b, i: (b, i, 0))
    tab = pl.BlockSpec((tm, LANES), lambda b, i: (i, 0))
    kv_spec = pl.BlockSpec((1, ATTN_KV_HEADS, tm, ATTN_DH), lambda b, i: (b, 0, i, 0))
    vt_spec = pl.BlockSpec((1, ATTN_KV_HEADS, VT_ROWS, tm), lambda b, i: (b, 0, 0, i))
    out_shape = (
        jax.ShapeDtypeStruct((bsz, n, D_MODEL), F32),
        jax.ShapeDtypeStruct((bsz, n, 2 * GLA_QK_W), F32),
        jax.ShapeDtypeStruct((bsz, n, GLA_V_W), BF16),
        jax.ShapeDtypeStruct((bsz, n, GLA_V_W), F32),
        jax.ShapeDtypeStruct((bsz, n, Z_W), F32),
        jax.ShapeDtypeStruct((bsz, n, ATTN_Q_W), BF16),
        jax.ShapeDtypeStruct((bsz, ATTN_KV_HEADS, n, ATTN_DH), BF16),
        jax.ShapeDtypeStruct((bsz, ATTN_KV_HEADS, VT_ROWS, n), BF16),
    )
    return pl.pallas_call(
        _kernel_a,
        grid=(bsz, n // tm),
        in_specs=[tok(D_MODEL),
                  _const_spec((D_MODEL, D_FF)), _const_spec((D_MODEL, D_FF)), _const_spec((D_FF, D_MODEL)),
                  _const_spec((1, D_MODEL)), _const_spec((1, D_MODEL)),
                  _const_spec((D_MODEL, D_IN)), _const_spec((AVG_W, AVG_W)),
                  _const_spec((1, ATTN_Q_W)), _const_spec((1, ATTN_KV_W)),
                  tab, tab, tab],
        out_specs=(tok(D_MODEL), tok(2 * GLA_QK_W), tok(GLA_V_W), tok(GLA_V_W), tok(Z_W), tok(ATTN_Q_W),
                   kv_spec, vt_spec),
        out_shape=out_shape,
        compiler_params=pltpu.CompilerParams(dimension_semantics=("arbitrary", "arbitrary"),
                                             vmem_limit_bytes=VMEM_LIMIT),
        name="ffn1_inproj",
    )(x, w["ffn1_wg"], w["ffn1_wu"], w["ffn1_wd"], w["ln1_g"], w["ln1_b"], w["w_in"], w["avg"],
      w["q_gain"], w["k_gain"], *tabs)


def _gla_log_decay_cumsum(z_ref, w2_ref, b2_ref, tri_ref):
    tn = z_ref.shape[1]
    pre = _dot(z_ref[0].astype(BF16), w2_ref[...]) + b2_ref[...]
    log_a = (jnp.minimum(pre, 0.0) - jnp.log(1.0 + jnp.exp(-jnp.abs(pre)))) * (1.0 / GLA_TAU)
    hi, lo = _split_bf16(log_a)
    tri = tri_ref[...]
    cum = jnp.concatenate(
        [_dot(tri, hi[r:r + CUM_TILE]) + _dot(tri, lo[r:r + CUM_TILE]) for r in range(0, tn, CUM_TILE)],
        axis=0)
    return cum * LOG2E


def _gla_chunk(gqk_ref, gv_ref, out_ref, cum, state, c, half_mask, keep, reverse):
    c_len = GLA_CHUNK
    r0 = c_len * c
    cc = cum[r0:r0 + c_len]
    mid = cc[c_len // 2:c_len // 2 + 1] if reverse else cc[c_len // 2 - 1:c_len // 2]
    last = cc[0:1] if reverse else cc[c_len - 1:c_len]
    qc = gqk_ref[0, r0:r0 + c_len, 0:GLA_QK_W] * GLA_SCALE
    kc = gqk_ref[0, r0:r0 + c_len, GLA_QK_W:2 * GLA_QK_W]
    vc = gv_ref[0, r0:r0 + c_len, :]
    rel = cc - mid
    q_in = qc * jnp.exp2(rel)
    k_in = (kc * jnp.exp2(-rel)).astype(BF16)
    k_up = kc * jnp.exp2(last - cc)
    q_st = qc * jnp.exp2(cc)
    state_b = state.astype(BF16)
    scores, o_state = [], []
    for col in range(GLA_QK_W // LANES):
        lanes = slice(LANES * col, LANES * (col + 1))
        stack = lambda x: jnp.concatenate([x[:, lanes] * m for m in half_mask], axis=0).astype(BF16)
        scores.append(_dot_nt(stack(q_in), k_in[:, lanes]))
        o_state.append(_dot(stack(q_st), state_b[lanes, :]))
    p = jnp.where(keep, jnp.concatenate(scores, axis=0), 0.0).astype(BF16)
    o_state = jnp.concatenate(o_state, axis=0)
    xt = jnp.concatenate([k_up, jnp.broadcast_to(last, (c_len, GLA_QK_W))], axis=0).T
    k_up_t = xt[:, 0:c_len].astype(BF16)
    decay_col = jnp.exp2(xt[:, c_len:c_len + 1])
    upd = []
    for h in range(GLA_HEADS):
        rows = slice(c_len * h, c_len * (h + 1))
        vh = vc[:, GLA_DV * h:GLA_DV * (h + 1)]
        out_ref[0, r0:r0 + c_len, GLA_DV * h:GLA_DV * (h + 1)] = _dot(p[rows], vh) + o_state[rows]
        upd.append(_dot(k_up_t[rows], vh))
    return decay_col * state + jnp.concatenate(upd, axis=0)


def _kernel_gla(gqkf_ref, gvf_ref, zf_ref, gqkb_ref, gvb_ref, zb_ref,
                w2f_ref, b2f_ref, w2b_ref, b2b_ref, lower_ref, upper_ref,
                of_ref, ob_ref, state_ref):
    @pl.when(pl.program_id(1) == 0)
    def _():
        state_ref[...] = jnp.zeros_like(state_ref)

    c_len = GLA_CHUNK
    n_chunks = gqkf_ref.shape[1] // c_len
    cum_f = _gla_log_decay_cumsum(zf_ref, w2f_ref, b2f_ref, lower_ref)
    cum_b = _gla_log_decay_cumsum(zb_ref, w2b_ref, b2b_ref, upper_ref)
    first_half = lax.broadcasted_iota(jnp.int32, (c_len, LANES), 1) < GLA_DK
    half_mask = [first_half.astype(F32), 1.0 - first_half.astype(F32)]
    row = lax.broadcasted_iota(jnp.int32, (GLA_HEADS * c_len, c_len), 0) % c_len
    col = lax.broadcasted_iota(jnp.int32, (GLA_HEADS * c_len, c_len), 1)
    state_f, state_b = state_ref[0], state_ref[1]
    for c in range(n_chunks):
        state_f = _gla_chunk(gqkf_ref, gvf_ref, of_ref, cum_f, state_f, c, half_mask, col <= row, False)
        state_b = _gla_chunk(gqkb_ref, gvb_ref, ob_ref, cum_b, state_b, n_chunks - 1 - c, half_mask,
                             col > row, True)
    state_ref[0] = state_f
    state_ref[1] = state_b


def _call_gla(gqk, gv, z, w):
    bsz, n, _ = gqk.shape
    tn = GLA_TILE
    nb = n // tn
    fwd = lambda width: pl.BlockSpec((1, tn, width), lambda b, j: (b, j, 0))
    bwd = lambda width: pl.BlockSpec((1, tn, width), lambda b, j: (b, nb - 1 - j, 0))
    o_shape = jax.ShapeDtypeStruct((bsz, n, GLA_V_W), F32)
    return pl.pallas_call(
        _kernel_gla,
        grid=(bsz, nb),
        in_specs=[fwd(2 * GLA_QK_W), fwd(GLA_V_W), fwd(Z_W), bwd(2 * GLA_QK_W), bwd(GLA_V_W), bwd(Z_W),
                  _const_spec((Z_W, GLA_QK_W)), _const_spec((1, GLA_QK_W)),
                  _const_spec((Z_W, GLA_QK_W)), _const_spec((1, GLA_QK_W)),
                  _const_spec((CUM_TILE, CUM_TILE)), _const_spec((CUM_TILE, CUM_TILE))],
        out_specs=(fwd(GLA_V_W), bwd(GLA_V_W)),
        out_shape=(o_shape, o_shape),
        scratch_shapes=[pltpu.VMEM((2, GLA_HEADS * GLA_DK, GLA_DV), F32)],
        compiler_params=pltpu.CompilerParams(dimension_semantics=("arbitrary", "arbitrary"),
                                             vmem_limit_bytes=VMEM_LIMIT),
        name="gla_bidir",
    )(gqk, gv, z, gqk, gv, z, w["w2f"], w["b2f"], w["w2b"], w["b2b"], w["tri_lower"], w["tri_upper"])


def _kernel_attn(flag_ref, q_ref, k_ref, vt_ref, o_ref, st_ref, m_ref):
    n = k_ref.shape[2]
    sub = ATTN_UNIT_ROWS
    hpu = ATTN_UNIT_HEADS
    width = hpu * sub
    units = [(s, hg) for s in range(q_ref.shape[1] // sub) for hg in range(ATTN_HEADS // ATTN_KV_HEADS // hpu)]
    n_units = len(units)
    tiles = [slice(t, t + ATTN_KEY_TILE) for t in range(0, n, ATTN_KEY_TILE)]

    def run_phase(phase):
        u1, u2 = phase, phase - 1
        if u1 < n_units:
            s, hg = units[u1]
            heads = [q_ref[0, sub * s:sub * (s + 1), ATTN_DH * h:ATTN_DH * (h + 1)]
                     for h in range(hpu * hg, hpu * (hg + 1))]
            q_cur = jnp.concatenate(heads, axis=0)
            m_run = None
        if u2 >= 0:
            m_fin = m_ref[u2]
            acc = jnp.zeros((VT_ROWS, width), F32)
        for tile in tiles:
            if u2 >= 0:
                pt = jnp.exp2(st_ref[u2 % 2, tile, :] - m_fin).astype(BF16)
                acc = acc + _dot(vt_ref[0, 0, :, tile], pt)
            if u1 < n_units:
                st = _dot_nt(k_ref[0, 0, tile, :], q_cur)
                st_ref[u1 % 2, tile, :] = st
                m_tile = jnp.max(st, axis=0, keepdims=True)
                m_run = m_tile if m_run is None else jnp.maximum(m_run, m_tile)
        if u1 < n_units:
            m_ref[u1] = m_run
        if u2 >= 0:
            s, hg = units[u2]
            o2 = (acc[0:ATTN_DH] / acc[ATTN_DH:ATTN_DH + 1]).T
            o_ref[0, sub * s:sub * (s + 1), ATTN_DH * hpu * hg:ATTN_DH * hpu * (hg + 1)] = (
                jnp.concatenate([o2[sub * i:sub * (i + 1)] for i in range(hpu)], axis=1).astype(BF16))

    for phase in range(n_units + 1):
        pl.when(flag_ref[phase] == 0)(functools.partial(run_phase, phase))

def _call_attn(q, k, vt):
    bsz, n, _ = q.shape
    tq = ATTN_Q_TILE
    group_w = ATTN_Q_W // ATTN_KV_HEADS
    width = ATTN_UNIT_HEADS * ATTN_UNIT_ROWS
    n_units = (tq // ATTN_UNIT_ROWS) * (group_w // (ATTN_UNIT_HEADS * ATTN_DH))
    q_spec = pl.BlockSpec((1, tq, group_w), lambda b, g, i: (b, i, g))
    k_spec = pl.BlockSpec((1, 1, n, ATTN_DH), lambda b, g, i: (b, g, 0, 0))
    vt_spec = pl.BlockSpec((1, 1, VT_ROWS, n), lambda b, g, i: (b, g, 0, 0))
    return pl.pallas_call(
        _kernel_attn,
        grid=(bsz, ATTN_KV_HEADS, n // tq),
        in_specs=[pl.BlockSpec(memory_space=pltpu.SMEM), q_spec, k_spec, vt_spec],
        out_specs=q_spec,
        out_shape=jax.ShapeDtypeStruct((bsz, n, ATTN_Q_W), BF16),
        scratch_shapes=[pltpu.VMEM((2, n, width), F32),
                        pltpu.VMEM((n_units, 1, width), F32)],
        compiler_params=pltpu.CompilerParams(
            dimension_semantics=("arbitrary", "arbitrary", "arbitrary"),
            vmem_limit_bytes=VMEM_LIMIT),
        name="gqa_attention",
    )(jnp.zeros((n_units + 1,), jnp.int32), q, k, vt)


def _kernel_b(h1_ref, of_ref, ob_ref, gr_ref, oa_ref, p_ref, gn_ref, wog_ref, woa_ref,
              ln2g_ref, ln2b_ref, wg_ref, wu_ref, wd_ref, ln3g_ref, ln3b_ref, wpg_ref, bpg_ref, wpe_ref,
              out_ref):
    o = of_ref[0] + ob_ref[0]
    gn = gn_ref[...]
    normed = []
    for h in range(GLA_HEADS):
        oh = o[:, GLA_DV * h:GLA_DV * (h + 1)]
        ms = jnp.mean(oh * oh, axis=-1, keepdims=True)
        normed.append(oh * lax.rsqrt(ms + GN_EPS) * gn)
    gr = gr_ref[0]
    o_gla = jnp.concatenate(normed, axis=1) * (gr * jax.nn.sigmoid(gr))
    mix = _dot(o_gla.astype(BF16), wog_ref[...]) + _dot(oa_ref[0], woa_ref[...])
    h2 = _layer_norm(DEEPNORM_ALPHA * h1_ref[0] + mix, ln2g_ref[...], ln2b_ref[...])
    f = _swiglu(h2.astype(BF16), wg_ref, wu_ref, wd_ref)
    h3 = _layer_norm(DEEPNORM_ALPHA * h2 + 0.5 * f, ln3g_ref[...], ln3b_ref[...])
    gate = jax.nn.sigmoid(_dot(h3.astype(BF16), wpg_ref[...]) + bpg_ref[...])
    out_ref[0] = h3 + gate * _dot(p_ref[0].astype(BF16), wpe_ref[...])


def _call_b(h1, o_f, o_b, gr, o_att, p, w):
    bsz, n, _ = h1.shape
    tm = TOKEN_TILE
    tok = lambda width: pl.BlockSpec((1, tm, width), lambda b, i: (b, i, 0))
    return pl.pallas_call(
        _kernel_b,
        grid=(bsz, n // tm),
        in_specs=[tok(D_MODEL), tok(GLA_V_W), tok(GLA_V_W), tok(GLA_V_W), tok(ATTN_Q_W), tok(P_DIM),
                  _const_spec((1, GLA_DV)),
                  _const_spec((GLA_V_W, D_MODEL)), _const_spec((ATTN_Q_W, D_MODEL)),
                  _const_spec((1, D_MODEL)), _const_spec((1, D_MODEL)),
                  _const_spec((D_MODEL, D_FF)), _const_spec((D_MODEL, D_FF)), _const_spec((D_FF, D_MODEL)),
                  _const_spec((1, D_MODEL)), _const_spec((1, D_MODEL)),
                  _const_spec((D_MODEL, D_MODEL)), _const_spec((1, D_MODEL)), _const_spec((P_DIM, D_MODEL))],
        out_specs=tok(D_MODEL),
        out_shape=jax.ShapeDtypeStruct((bsz, n, D_MODEL), F32),
        compiler_params=pltpu.CompilerParams(dimension_semantics=("arbitrary", "arbitrary"),
                                             vmem_limit_bytes=VMEM_LIMIT),
        name="outproj_ffn2_embed",
    )(h1, o_f, o_b, gr, o_att, p, w["gn_g"], w["w_out_gla"], w["w_out_att"], w["ln2_g"], w["ln2_b"],
      w["ffn2_wg"], w["ffn2_wu"], w["ffn2_wd"], w["ln3_g"], w["ln3_b"], w["w_pg"], w["b_pg"], w["w_pe"])


def _rope_tables(n):
    t = jnp.arange(n, dtype=jnp.int32)
    row = (t // GRID_W).astype(F32)
    col = (t % GRID_W).astype(F32)
    axis_dim = ATTN_DH // 2
    inv_freq = ROPE_THETA ** (-jnp.arange(0, axis_dim, 2, dtype=F32) / axis_dim)
    lane = jnp.arange(LANES, dtype=jnp.int32) % ATTN_DH
    freq = inv_freq[lane % ROPE_HALF]
    pos = jnp.where((lane // axis_dim)[None, :] == 0, row[:, None], col[:, None])
    ang = pos * freq[None, :]
    first_half = ((lane % axis_dim) // ROPE_HALF == 0)[None, :]
    cos, sin = jnp.cos(ang), jnp.sin(ang)
    return cos, jnp.where(first_half, -sin, 0.0), jnp.where(first_half, 0.0, sin)


def _prepare(ffn1_wg, ffn1_wu, ffn1_wd, ln1_g, ln1_b, w_in, gla_w2f, gla_b2f, gla_w2b, gla_b2b,
             gla_gn_g, q_norm_g, k_norm_g, w_out, ln2_g, ln2_b, ffn2_wg, ffn2_wu, ffn2_wd,
             ln3_g, ln3_b, w_pg, b_pg, w_pe):
    row = lambda v: v.reshape(1, -1).astype(F32)
    z0 = 2 * GLA_QK_W + 2 * GLA_V_W
    w_in_r = jnp.concatenate([w_in[:, :z0], w_in[:, z0 + Z_W:], w_in[:, z0:z0 + Z_W]], axis=1)
    zeros = jnp.zeros((GLA_GATE_RANK, GLA_QK_W), F32)
    idx = jnp.arange(AVG_W)
    avg =jnp.where((idx[:, None] // ATTN_DH) == (idx[None, :] // ATTN_DH), 1.0 / ATTN_DH, 0.0)
    ci = jnp.arange(CUM_TILE)
    same_chunk = (ci[:, None] // GLA_CHUNK) == (ci[None, :] // GLA_CHUNK)
    lower = jnp.where(same_chunk & (ci[None, :] <= ci[:, None]), 1.0, 0.0)
    return {
        "ffn1_wg": ffn1_wg.astype(BF16), "ffn1_wu": ffn1_wu.astype(BF16), "ffn1_wd": ffn1_wd.astype(BF16),
        "ln1_g": row(ln1_g), "ln1_b": row(ln1_b),
        "w_in": w_in_r.astype(BF16), "avg": avg.astype(BF16),
        "q_gain": row(jnp.tile(q_norm_g, ATTN_HEADS)), "k_gain": row(jnp.tile(k_norm_g, ATTN_KV_HEADS)),
        "w2f": jnp.concatenate([gla_w2f, zeros], axis=0).astype(BF16), "b2f": row(gla_b2f),
        "w2b": jnp.concatenate([zeros, gla_w2b], axis=0).astype(BF16), "b2b": row(gla_b2b),
        "tri_lower": lower.astype(BF16), "tri_upper": lower.T.astype(BF16),
        "gn_g": row(gla_gn_g),
        "w_out_gla": w_out[:GLA_V_W].astype(BF16), "w_out_att": w_out[GLA_V_W:].astype(BF16),
        "ln2_g": row(ln2_g), "ln2_b": row(ln2_b),
        "ffn2_wg": ffn2_wg.astype(BF16), "ffn2_wu": ffn2_wu.astype(BF16), "ffn2_wd": ffn2_wd.astype(BF16),
        "ln3_g": row(ln3_g), "ln3_b": row(ln3_b),
        "w_pg": w_pg.astype(BF16), "b_pg": row(b_pg), "w_pe": w_pe.astype(BF16),
    }


def _encoder_layer(x, p, w):
    n = x.shape[1]
    assert n % GLA_TILE == 0 and n % ATTN_Q_TILE == 0 and n % TOKEN_TILE == 0 and n % GRID_W == 0
    h1, gqk, gv, gr, z, q, k, v = _call_a(x, w, _rope_tables(n))
    o_f, o_b = _call_gla(gqk, gv, z, w)
    o_att = _call_attn(q, k, v)
    return _call_b(h1, o_f, o_b, gr, o_att, p, w)


def kernel(x_prompt, x_sample, p_prompt, p_sample, ffn1_wg, ffn1_wu, ffn1_wd, ln1_g, ln1_b, w_in,
           gla_w2f, gla_b2f, gla_w2b, gla_b2b, gla_gn_g, q_norm_g, k_norm_g, w_out, ln2_g, ln2_b,
           ffn2_wg, ffn2_wu, ffn2_wd, ln3_g, ln3_b, w_pg, b_pg, w_pe):
    y_prompt, y_sample = x_prompt, x_sample
    for i in range(DEPTH):
        w = _prepare(ffn1_wg[i], ffn1_wu[i], ffn1_wd[i], ln1_g[i], ln1_b[i], w_in[i],
                     gla_w2f[i], gla_b2f[i], gla_w2b[i], gla_b2b[i], gla_gn_g[i], q_norm_g[i],
                     k_norm_g[i], w_out[i], ln2_g[i], ln2_b[i], ffn2_wg[i], ffn2_wu[i], ffn2_wd[i],
                     ln3_g[i], ln3_b[i], w_pg[i], b_pg[i], w_pe[i])
        y_prompt = _encoder_layer(y_prompt, p_prompt[i], w)
        y_sample = _encoder_layer(y_sample, p_sample[i], w)
    return (y_prompt, y_sample)
```

```python
import functools

import jax
import jax.numpy as jnp
from jax import lax
from jax.experimental import pallas as pl
from jax.experimental.pallas import tpu as pltpu

F32 = jnp.float32
BF16 = jnp.bfloat16

D_MODEL = 1024
D_FF = 2816
P_DIM = 256
GRID_W = 64
DEPTH = 1
GLA_HEADS = 4
GLA_DK = 64
GLA_DV = 128
GLA_GATE_RANK = 16
GLA_TAU = 16.0
GLA_CHUNK = 64
ATTN_HEADS = 8
ATTN_KV_HEADS = 2
ATTN_DH = 64
ROPE_THETA = 10000.0
LN_EPS = 1e-5
QK_EPS = 1e-6
GN_EPS = 1e-5
DEEPNORM_ALPHA = (2.0 * DEPTH) ** 0.25

GLA_QK_W = GLA_HEADS * GLA_DK
GLA_V_W = GLA_HEADS * GLA_DV
ATTN_Q_W = ATTN_HEADS * ATTN_DH
ATTN_KV_W = ATTN_KV_HEADS * ATTN_DH
Z_W = 2 * GLA_GATE_RANK
OFF_GQK = 0
OFF_GV = OFF_GQK + 2 * GLA_QK_W
OFF_GR = OFF_GV + GLA_V_W
OFF_AQ = OFF_GR + GLA_V_W
OFF_AK = OFF_AQ + ATTN_Q_W
OFF_AV = OFF_AK + ATTN_KV_W
OFF_Z = OFF_AV + ATTN_KV_W
D_IN = OFF_Z + Z_W

LANES = 128
AVG_W = 256
ROPE_HALF = ATTN_DH // 4
LOG2E = 1.4426950408889634
QK_SCALE = ATTN_DH ** -0.5 * LOG2E
GLA_SCALE = GLA_DK ** -0.5

TOKEN_TILE = 512
FF_CHUNKS = ((0, 1536), (1536, D_FF))
GLA_TILE = 1024
CUM_TILE = 256
ATTN_Q_TILE = 1024
ATTN_UNIT_ROWS = 256
ATTN_UNIT_HEADS = 4
ATTN_KEY_TILE = 512
VT_ROWS = ATTN_DH + 16
VMEM_LIMIT = 56 * 1024 * 1024


def _const_spec(shape):
    return pl.BlockSpec(shape, lambda *_: (0,) * len(shape), pipeline_mode=pl.Buffered(1))


def _dot(a, b):
    return jnp.dot(a, b, preferred_element_type=F32)


def _dot_nt(a, b):
    return lax.dot_general(a, b, (((1,), (1,)), ((), ())), preferred_element_type=F32)


def _split_bf16(x):
    hi = x.astype(BF16)
    lo = (x - hi.astype(F32)).astype(BF16)
    return hi, lo


def _layer_norm(y, g, b):
    mu = jnp.mean(y, axis=-1, keepdims=True)
    d = y - mu
    var = jnp.mean(d * d, axis=-1, keepdims=True)
    return d * lax.rsqrt(var + LN_EPS) * g + b


def _swiglu(xb, wg_ref, wu_ref, wd_ref):
    out = None
    for lo, hi in FF_CHUNKS:
        g = _dot(xb, wg_ref[:, lo:hi])
        u = _dot(xb, wu_ref[:, lo:hi])
        hid = (g * jax.nn.sigmoid(g)) * u
        part = _dot(hid.astype(BF16), wd_ref[lo:hi, :])
        out = part if out is None else out + part
    return out


def _head_rms(x, avg, gain, eps):
    hi, lo = _split_bf16(x * x)
    w = min(x.shape[1], AVG_W)
    a = avg[:w, :w]
    ms = jnp.concatenate([_dot(hi[:, c:c + w], a) + _dot(lo[:, c:c + w], a) for c in range(0, x.shape[1], w)],
                         axis=1)
    return x * lax.rsqrt(ms + eps) * gain


def _rope(xs, c, sa, sb):
    return (xs * c + pltpu.roll(xs, LANES - ROPE_HALF, 1) * sa
            + pltpu.roll(xs, ROPE_HALF, 1) * sb)


def _kernel_a(x_ref, wg_ref, wu_ref, wd_ref, lng_ref, lnb_ref, win_ref, avg_ref, qg_ref, kg_ref,
              c_ref, sa_ref, sb_ref,
              h_ref, gqk_ref, gv_ref, gr_ref, z_ref, q_ref, k_ref, v_ref):
    x = x_ref[0]
    f = _swiglu(x.astype(BF16), wg_ref, wu_ref, wd_ref)
    h = _layer_norm(DEEPNORM_ALPHA * x + 0.5 * f, lng_ref[...], lnb_ref[...])
    h_ref[0] = h
    proj = _dot(h.astype(BF16), win_ref[...])
    gqk_ref[0] = proj[:, OFF_GQK:OFF_GV]
    gv_ref[0] = proj[:, OFF_GV:OFF_GR].astype(BF16)
    gr_ref[0] = proj[:, OFF_GR:OFF_AQ]
    z_ref[0] = proj[:, OFF_Z:D_IN]
    c, sa, sb = c_ref[...], sa_ref[...], sb_ref[...]
    avg = avg_ref[...]
    qn = _head_rms(proj[:, OFF_AQ:OFF_AK], avg, qg_ref[...], QK_EPS)
    for j in range(ATTN_Q_W // LANES):
        sl = slice(LANES * j, LANES * (j + 1))
        q_ref[0, :, sl] = (_rope(qn[:, sl], c, sa, sb) * QK_SCALE).astype(BF16)
    kn = _head_rms(proj[:, OFF_AK:OFF_AV], avg, kg_ref[...], QK_EPS)
    kr = _rope(kn, c, sa, sb).astype(BF16)
    vt = proj[:, OFF_AV:OFF_Z].T.astype(BF16)
    ones = jnp.ones((VT_ROWS - ATTN_DH, vt.shape[1]), BF16)
    for g in range(ATTN_KV_HEADS):
        k_ref[0, g] = kr[:, ATTN_DH * g:ATTN_DH * (g + 1)]
        v_ref[0, g, 0:ATTN_DH, :] = vt[ATTN_DH * g:ATTN_DH * (g + 1), :]
        v_ref[0, g, ATTN_DH:VT_ROWS, :] = ones


def _call_a(x, w, tabs):
    bsz, n, _ = x.shape
    tm = TOKEN_TILE
    tok = lambda width: pl.BlockSpec((1, tm, width), lambda b, i: (b, i, 0))
    tab = pl.BlockSpec((tm, LANES), lambda b, i: (i, 0))
    kv_spec = pl.BlockSpec((1, ATTN_KV_HEADS, tm, ATTN_DH), lambda b, i: (b, 0, i, 0))
    vt_spec = pl.BlockSpec((1, ATTN_KV_HEADS, VT_ROWS, tm), lambda b, i: (b, 0, 0, i))
    out_shape = (
        jax.ShapeDtypeStruct((bsz, n, D_MODEL), F32),
        jax.ShapeDtypeStruct((bsz, n, 2 * GLA_QK_W), F32),
        jax.ShapeDtypeStruct((bsz, n, GLA_V_W), BF16),
        jax.ShapeDtypeStruct((bsz, n, GLA_V_W), F32),
        jax.ShapeDtypeStruct((bsz, n, Z_W), F32),
        jax.ShapeDtypeStruct((bsz, n, ATTN_Q_W), BF16),
        jax.ShapeDtypeStruct((bsz, ATTN_KV_HEADS, n, ATTN_DH), BF16),
        jax.ShapeDtypeStruct((bsz, ATTN_KV_HEADS, VT_ROWS, n), BF16),
    )
    return pl.pallas_call(
        _kernel_a,
        grid=(bsz, n // tm),
        in_specs=[tok(D_MODEL),
                  _const_spec((D_MODEL, D_FF)), _const_spec((D_MODEL, D_FF)), _const_spec((D_FF, D_MODEL)),
                  _const_spec((1, D_MODEL)), _const_spec((1, D_MODEL)),
                  _const_spec((D_MODEL, D_IN)), _const_spec((AVG_W, AVG_W)),
                  _const_spec((1, ATTN_Q_W)), _const_spec((1, ATTN_KV_W)),
                  tab, tab, tab],
        out_specs=(tok(D_MODEL), tok(2 * GLA_QK_W), tok(GLA_V_W), tok(GLA_V_W), tok(Z_W), tok(ATTN_Q_W),
                   kv_spec, vt_spec),
        out_shape=out_shape,
        compiler_params=pltpu.CompilerParams(dimension_semantics=("arbitrary", "arbitrary"),
                                             vmem_limit_bytes=VMEM_LIMIT),
        name="ffn1_inproj",
    )(x, w["ffn1_wg"], w["ffn1_wu"], w["ffn1_wd"], w["ln1_g"], w["ln1_b"], w["w_in"], w["avg"],
      w["q_gain"], w["k_gain"], *tabs)


def _gla_log_decay_cumsum(z_ref, w2_ref, b2_ref, tri_ref):
    tn = z_ref.shape[1]
    pre = _dot(z_ref[0].astype(BF16), w2_ref[...]) + b2_ref[...]
    log_a = (jnp.minimum(pre, 0.0) - jnp.log(1.0 + jnp.exp(-jnp.abs(pre)))) * (1.0 / GLA_TAU)
    hi, lo = _split_bf16(log_a)
    tri = tri_ref[...]
    cum = jnp.concatenate(
        [_dot(tri, hi[r:r + CUM_TILE]) + _dot(tri, lo[r:r + CUM_TILE]) for r in range(0, tn, CUM_TILE)],
        axis=0)
    return cum * LOG2E


def _gla_chunk(gqk_ref, gv_ref, out_ref, cum, state, c, half_mask, keep, reverse):
    c_len = GLA_CHUNK
    r0 = c_len * c
    cc = cum[r0:r0 + c_len]
    mid = cc[c_len // 2:c_len // 2 + 1] if reverse else cc[c_len // 2 - 1:c_len // 2]
    last = cc[0:1] if reverse else cc[c_len - 1:c_len]
    qc = gqk_ref[0, r0:r0 + c_len, 0:GLA_QK_W] * GLA_SCALE
    kc = gqk_ref[0, r0:r0 + c_len, GLA_QK_W:2 * GLA_QK_W]
    vc = gv_ref[0, r0:r0 + c_len, :]
    rel = cc - mid
    q_in = qc * jnp.exp2(rel)
    k_in = (kc * jnp.exp2(-rel)).astype(BF16)
    k_up = kc * jnp.exp2(last - cc)
    q_st = qc * jnp.exp2(cc)
    state_b = state.astype(BF16)
    scores, o_state = [], []
    for col in range(GLA_QK_W // LANES):
        lanes = slice(LANES * col, LANES * (col + 1))
        stack = lambda x: jnp.concatenate([x[:, lanes] * m for m in half_mask], axis=0).astype(BF16)
        scores.append(_dot_nt(stack(q_in), k_in[:, lanes]))
        o_state.append(_dot(stack(q_st), state_b[lanes, :]))
    p = jnp.where(keep, jnp.concatenate(scores, axis=0), 0.0).astype(BF16)
    o_state = jnp.concatenate(o_state, axis=0)
    xt = jnp.concatenate([k_up, jnp.broadcast_to(last, (c_len, GLA_QK_W))], axis=0).T
    k_up_t = xt[:, 0:c_len].astype(BF16)
    decay_col = jnp.exp2(xt[:, c_len:c_len + 1])
    upd = []
    for h in range(GLA_HEADS):
        rows = slice(c_len * h, c_len * (h + 1))
        vh = vc[:, GLA_DV * h:GLA_DV * (h + 1)]
        out_ref[0, r0:r0 + c_len, GLA_DV * h:GLA_DV * (h + 1)] = _dot(p[rows], vh) + o_state[rows]
        upd.append(_dot(k_up_t[rows], vh))
    return decay_col * state + jnp.concatenate(upd, axis=0)


def _kernel_gla(gqkf_ref, gvf_ref, zf_ref, gqkb_ref, gvb_ref, zb_ref,
                w2f_ref, b2f_ref, w2b_ref, b2b_ref, lower_ref, upper_ref,
                of_ref, ob_ref, state_ref):
    @pl.when(pl.program_id(1) == 0)
    def _():
        state_ref[...] = jnp.zeros_like(state_ref)

    c_len = GLA_CHUNK
    n_chunks = gqkf_ref.shape[1] // c_len
    cum_f = _gla_log_decay_cumsum(zf_ref, w2f_ref, b2f_ref, lower_ref)
    cum_b = _gla_log_decay_cumsum(zb_ref, w2b_ref, b2b_ref, upper_ref)
    first_half = lax.broadcasted_iota(jnp.int32, (c_len, LANES), 1) < GLA_DK
    half_mask = [first_half.astype(F32), 1.0 - first_half.astype(F32)]
    row = lax.broadcasted_iota(jnp.int32, (GLA_HEADS * c_len, c_len), 0) % c_len
    col = lax.broadcasted_iota(jnp.int32, (GLA_HEADS * c_len, c_len), 1)
    state_f, state_b = state_ref[0], state_ref[1]
    for c in range(n_chunks):
        state_f = _gla_chunk(gqkf_ref, gvf_ref, of_ref, cum_f, state_f, c, half_mask, col <= row, False)
        state_b = _gla_chunk(gqkb_ref, gvb_ref, ob_ref, cum_b, state_b, n_chunks - 1 - c, half_mask,
                             col > row, True)
    state_ref[0] = state_f
    state_ref[1] = state_b


def _call_gla(gqk, gv, z, w):
    bsz, n, _ = gqk.shape
    tn = GLA_TILE
    nb = n // tn
    fwd = lambda width: pl.BlockSpec((1, tn, width), lambda b, j: (b, j, 0))
    bwd = lambda width: pl.BlockSpec((1, tn, width), lambda b, j: (b, nb - 1 - j, 0))
    o_shape = jax.ShapeDtypeStruct((bsz, n, GLA_V_W), F32)
    return pl.pallas_call(
        _kernel_gla,
        grid=(bsz, nb),
        in_specs=[fwd(2 * GLA_QK_W), fwd(GLA_V_W), fwd(Z_W), bwd(2 * GLA_QK_W), bwd(GLA_V_W), bwd(Z_W),
                  _const_spec((Z_W, GLA_QK_W)), _const_spec((1, GLA_QK_W)),
                  _const_spec((Z_W, GLA_QK_W)), _const_spec((1, GLA_QK_W)),
                  _const_spec((CUM_TILE, CUM_TILE)), _const_spec((CUM_TILE, CUM_TILE))],
        out_specs=(fwd(GLA_V_W), bwd(GLA_V_W)),
        out_shape=(o_shape, o_shape),
        scratch_shapes=[pltpu.VMEM((2, GLA_HEADS * GLA_DK, GLA_DV), F32)],
        compiler_params=pltpu.CompilerParams(dimension_semantics=("arbitrary", "arbitrary"),
                                             vmem_limit_bytes=VMEM_LIMIT),
        name="gla_bidir",
    )(gqk, gv, z, gqk, gv, z, w["w2f"], w["b2f"], w["w2b"], w["b2b"], w["tri_lower"], w["tri_upper"])


def _kernel_attn(flag_ref, q_ref, k_ref, vt_ref, o_ref, st_ref, m_ref):
    n = k_ref.shape[2]
    sub = ATTN_UNIT_ROWS
    hpu = ATTN_UNIT_HEADS
    width = hpu * sub
    groups = ATTN_HEADS // ATTN_KV_HEADS // hpu
    n_units = (ATTN_Q_TILE // sub) * groups
    tiles = [slice(t, t + ATTN_KEY_TILE) for t in range(0, n, ATTN_KEY_TILE)]
    step = pl.program_id(2)
    last_step = pl.num_programs(2) - 1

    def unit_window(at_step, u):
        s, hg = divmod(u, groups)
        row0 = pl.multiple_of(at_step * ATTN_Q_TILE + sub * s, sub)
        return row0, hg

    def run_phase(u_scores, scores_step, u_pv):
        if u_scores is not None:
            row0, hg = unit_window(scores_step, u_scores)
            heads = [q_ref[0, pl.ds(row0, sub), ATTN_DH * h:ATTN_DH * (h + 1)]
                     for h in range(hpu * hg, hpu * (hg + 1))]
            q_cur = jnp.concatenate(heads, axis=0)
            m_run = None
        if u_pv is not None:
            m_fin = m_ref[u_pv]
            acc = jnp.zeros((VT_ROWS, width), F32)
        for tile in tiles:
            if u_pv is not None:
                pt = jnp.exp2(st_ref[u_pv % 2, tile, :] - m_fin).astype(BF16)
                acc = acc + _dot(vt_ref[0, 0, :, tile], pt)
            if u_scores is not None:
                st = _dot_nt(k_ref[0, 0, tile, :], q_cur)
                st_ref[u_scores % 2, tile, :] = st
                m_tile = jnp.max(st, axis=0, keepdims=True)
                m_run = m_tile if m_run is None else jnp.maximum(m_run, m_tile)
        if u_scores is not None:
            m_ref[u_scores] = m_run
        if u_pv is not None:
            row0, hg = unit_window(step, u_pv)
            o2 = (acc[0:ATTN_DH] / acc[ATTN_DH:ATTN_DH + 1]).T
            o_ref[0, pl.ds(row0, sub), ATTN_DH * hpu * hg:ATTN_DH * hpu * (hg + 1)] = (
                jnp.concatenate([o2[sub * i:sub * (i + 1)] for i in range(hpu)], axis=1).astype(BF16))

    pl.when(step == 0)(functools.partial(run_phase, 0, step, None))
    for u in range(n_units - 1):
        pl.when(flag_ref[u] == 0)(functools.partial(run_phase, u + 1, step, u))
    pl.when(step < last_step)(functools.partial(run_phase, 0, step + 1, n_units - 1))
    pl.when(step == last_step)(functools.partial(run_phase, None, None, n_units - 1))


def _call_attn(q, k, vt):
    bsz, n, _ = q.shape
    group_w = ATTN_Q_W // ATTN_KV_HEADS
    width = ATTN_UNIT_HEADS * ATTN_UNIT_ROWS
    n_units = (ATTN_Q_TILE // ATTN_UNIT_ROWS) * (group_w // (ATTN_UNIT_HEADS * ATTN_DH))
    assert n_units % 2 == 0
    q_spec = pl.BlockSpec((1, n, group_w), lambda b, g, i: (b, 0, g))
    k_spec = pl.BlockSpec((1, 1, n, ATTN_DH), lambda b, g, i: (b, g, 0, 0))
    vt_spec = pl.BlockSpec((1, 1, VT_ROWS, n), lambda b, g, i: (b, g, 0, 0))
    return pl.pallas_call(
        _kernel_attn,
        grid=(bsz, ATTN_KV_HEADS, n // ATTN_Q_TILE),
        in_specs=[pl.BlockSpec(memory_space=pltpu.SMEM), q_spec, k_spec, vt_spec],
        out_specs=q_spec,
        out_shape=jax.ShapeDtypeStruct((bsz, n, ATTN_Q_W), BF16),
        scratch_shapes=[pltpu.VMEM((2, n, width), F32),
                        pltpu.VMEM((n_units, 1, width), F32)],
        compiler_params=pltpu.CompilerParams(
            dimension_semantics=("arbitrary", "arbitrary", "arbitrary"),
            vmem_limit_bytes=VMEM_LIMIT),
        name="gqa_attention",
    )(jnp.zeros((n_units,), jnp.int32), q, k, vt)


def _kernel_b(h1_ref, of_ref, ob_ref, gr_ref, oa_ref, p_ref, gn_ref, wog_ref, woa_ref,
              ln2g_ref, ln2b_ref, wg_ref, wu_ref, wd_ref, ln3g_ref, ln3b_ref, wpg_ref, bpg_ref, wpe_ref,
              out_ref):
    o = of_ref[0] + ob_ref[0]
    gn = gn_ref[...]
    normed = []
    for h in range(GLA_HEADS):
        oh = o[:, GLA_DV * h:GLA_DV * (h + 1)]
        ms = jnp.mean(oh * oh, axis=-1, keepdims=True)
        normed.append(oh * lax.rsqrt(ms + GN_EPS) * gn)
    gr = gr_ref[0]
    o_gla = jnp.concatenate(normed, axis=1) * (gr * jax.nn.sigmoid(gr))
    mix = _dot(o_gla.astype(BF16), wog_ref[...]) + _dot(oa_ref[0], woa_ref[...])
    h2 = _layer_norm(DEEPNORM_ALPHA * h1_ref[0] + mix, ln2g_ref[...], ln2b_ref[...])
    f = _swiglu(h2.astype(BF16), wg_ref, wu_ref, wd_ref)
    h3 = _layer_norm(DEEPNORM_ALPHA * h2 + 0.5 * f, ln3g_ref[...], ln3b_ref[...])
    gate = jax.nn.sigmoid(_dot(h3.astype(BF16), wpg_ref[...]) + bpg_ref[...])
    out_ref[0] = h3 + gate * _dot(p_ref[0].astype(BF16), wpe_ref[...])


def _call_b(h1, o_f, o_b, gr, o_att, p, w):
    bsz, n, _ = h1.shape
    tm = TOKEN_TILE
    tok = lambda width: pl.BlockSpec((1, tm, width), lambda b, i: (b, i, 0))
    return pl.pallas_call(
        _kernel_b,
        grid=(bsz, n // tm),
        in_specs=[tok(D_MODEL), tok(GLA_V_W), tok(GLA_V_W), tok(GLA_V_W), tok(ATTN_Q_W), tok(P_DIM),
                  _const_spec((1, GLA_DV)),
                  _const_spec((GLA_V_W, D_MODEL)), _const_spec((ATTN_Q_W, D_MODEL)),
                  _const_spec((1, D_MODEL)), _const_spec((1, D_MODEL)),
                  _const_spec((D_MODEL, D_FF)), _const_spec((D_MODEL, D_FF)), _const_spec((D_FF, D_MODEL)),
                  _const_spec((1, D_MODEL)), _const_spec((1, D_MODEL)),
                  _const_spec((D_MODEL, D_MODEL)), _const_spec((1, D_MODEL)), _const_spec((P_DIM, D_MODEL))],
        out_specs=tok(D_MODEL),
        out_shape=jax.ShapeDtypeStruct((bsz, n, D_MODEL), F32),
        compiler_params=pltpu.CompilerParams(dimension_semantics=("arbitrary", "arbitrary"),
                                             vmem_limit_bytes=VMEM_LIMIT),
        name="outproj_ffn2_embed",
    )(h1, o_f, o_b, gr, o_att, p, w["gn_g"], w["w_out_gla"], w["w_out_att"], w["ln2_g"], w["ln2_b"],
      w["ffn2_wg"], w["ffn2_wu"], w["ffn2_wd"], w["ln3_g"], w["ln3_b"], w["w_pg"], w["b_pg"], w["w_pe"])


def _rope_tables(n):
    t = jnp.arange(n, dtype=jnp.int32)
    row = (t // GRID_W).astype(F32)
    col = (t % GRID_W).astype(F32)
    axis_dim = ATTN_DH // 2
    inv_freq = ROPE_THETA ** (-jnp.arange(0, axis_dim, 2, dtype=F32) / axis_dim)
    lane = jnp.arange(LANES, dtype=jnp.int32) % ATTN_DH
    freq = inv_freq[lane % ROPE_HALF]
    pos = jnp.where((lane // axis_dim)[None, :] == 0, row[:, None], col[:, None])
    ang = pos * freq[None, :]
    first_half = ((lane % axis_dim) // ROPE_HALF == 0)[None, :]
    cos, sin = jnp.cos(ang), jnp.sin(ang)
    return cos, jnp.where(first_half, -sin, 0.0), jnp.where(first_half, 0.0, sin)


def _prepare(ffn1_wg, ffn1_wu, ffn1_wd, ln1_g, ln1_b, w_in, gla_w2f, gla_b2f, gla_w2b, gla_b2b,
             gla_gn_g, q_norm_g, k_norm_g, w_out, ln2_g, ln2_b, ffn2_wg, ffn2_wu, ffn2_wd,
             ln3_g, ln3_b, w_pg, b_pg, w_pe):
    row = lambda v: v.reshape(1, -1).astype(F32)
    z0 = 2 * GLA_QK_W + 2 * GLA_V_W
    w_in_r = jnp.concatenate([w_in[:, :z0], w_in[:, z0 + Z_W:], w_in[:, z0:z0 + Z_W]], axis=1)
    zeros = jnp.zeros((GLA_GATE_RANK, GLA_QK_W), F32)
    idx = jnp.arange(AVG_W)
    avg =jnp.where((idx[:, None] // ATTN_DH) == (idx[None, :] // ATTN_DH), 1.0 / ATTN_DH, 0.0)
    ci = jnp.arange(CUM_TILE)
    same_chunk = (ci[:, None] // GLA_CHUNK) == (ci[None, :] // GLA_CHUNK)
    lower = jnp.where(same_chunk & (ci[None, :] <= ci[:, None]), 1.0, 0.0)
    return {
        "ffn1_wg": ffn1_wg.astype(BF16), "ffn1_wu": ffn1_wu.astype(BF16), "ffn1_wd": ffn1_wd.astype(BF16),
        "ln1_g": row(ln1_g), "ln1_b": row(ln1_b),
        "w_in": w_in_r.astype(BF16), "avg": avg.astype(BF16),
        "q_gain": row(jnp.tile(q_norm_g, ATTN_HEADS)), "k_gain": row(jnp.tile(k_norm_g, ATTN_KV_HEADS)),
        "w2f": jnp.concatenate([gla_w2f, zeros], axis=0).astype(BF16), "b2f": row(gla_b2f),
        "w2b": jnp.concatenate([zeros, gla_w2b], axis=0).astype(BF16), "b2b": row(gla_b2b),
        "tri_lower": lower.astype(BF16), "tri_upper": lower.T.astype(BF16),
        "gn_g": row(gla_gn_g),
        "w_out_gla": w_out[:GLA_V_W].astype(BF16), "w_out_att": w_out[GLA_V_W:].astype(BF16),
        "ln2_g": row(ln2_g), "ln2_b": row(ln2_b),
        "ffn2_wg": ffn2_wg.astype(BF16), "ffn2_wu": ffn2_wu.astype(BF16), "ffn2_wd": ffn2_wd.astype(BF16),
        "ln3_g": row(ln3_g), "ln3_b": row(ln3_b),
        "w_pg": w_pg.astype(BF16), "b_pg": row(b_pg), "w_pe": w_pe.astype(BF16),
    }


def _encoder_layer(x, p, w):
    n = x.shape[1]
    assert n % GLA_TILE == 0 and n % ATTN_Q_TILE == 0 and n % TOKEN_TILE == 0 and n % GRID_W == 0
    h1, gqk, gv, gr, z, q, k, v = _call_a(x, w, _rope_tables(n))
    o_f, o_b = _call_gla(gqk, gv, z, w)
    o_att = _call_attn(q, k, v)
    return _call_b(h1, o_f, o_b, gr, o_att, p, w)


def kernel(x_prompt, x_sample, p_prompt, p_sample, ffn1_wg, ffn1_wu, ffn1_wd, ln1_g, ln1_b, w_in,
           gla_w2f, gla_b2f, gla_w2b, gla_b2b, gla_gn_g, q_norm_g, k_norm_g, w_out, ln2_g, ln2_b,
           ffn2_wg, ffn2_wu, ffn2_wd, ln3_g, ln3_b, w_pg, b_pg, w_pe):
    y_prompt, y_sample = x_prompt, x_sample
    for i in range(DEPTH):
        w = _prepare(ffn1_wg[i], ffn1_wu[i], ffn1_wd[i], ln1_g[i], ln1_b[i], w_in[i],
                     gla_w2f[i], gla_b2f[i], gla_w2b[i], gla_b2b[i], gla_gn_g[i], q_norm_g[i],
                     k_norm_g[i], w_out[i], ln2_g[i], ln2_b[i], ffn2_wg[i], ffn2_wu[i], ffn2_wd[i],
                     ln3_g[i], ln3_b[i], w_pg[i], b_pg[i], w_pe[i])
        y_prompt = _encoder_layer(y_prompt, p_prompt[i], w)
        y_sample = _encoder_layer(y_sample, p_sample[i], w)
    return (y_prompt, y_sample)
```

```python
import functools

import jax
import jax.numpy as jnp
from jax import lax
from jax.experimental import pallas as pl
from jax.experimental.pallas import tpu as pltpu

F32 = jnp.float32
BF16 = jnp.bfloat16

D_MODEL = 1024
D_FF = 2816
P_DIM = 256
GRID_W = 64
DEPTH = 1
GLA_HEADS = 4
GLA_DK = 64
GLA_DV = 128
GLA_GATE_RANK = 16
GLA_TAU = 16.0
GLA_CHUNK = 64
ATTN_HEADS = 8
ATTN_KV_HEADS = 2
ATTN_DH = 64
ROPE_THETA = 10000.0
LN_EPS = 1e-5
QK_EPS = 1e-6
GN_EPS = 1e-5
DEEPNORM_ALPHA = (2.0 * DEPTH) ** 0.25

GLA_QK_W = GLA_HEADS * GLA_DK
GLA_V_W = GLA_HEADS * GLA_DV
ATTN_Q_W = ATTN_HEADS * ATTN_DH
ATTN_KV_W = ATTN_KV_HEADS * ATTN_DH
Z_W = 2 * GLA_GATE_RANK
OFF_GQK = 0
OFF_GV = OFF_GQK + 2 * GLA_QK_W
OFF_GR = OFF_GV + GLA_V_W
OFF_AQ = OFF_GR + GLA_V_W
OFF_AK = OFF_AQ + ATTN_Q_W
OFF_AV = OFF_AK + ATTN_KV_W
OFF_Z = OFF_AV + ATTN_KV_W
D_IN = OFF_Z + Z_W

LANES = 128
AVG_W = 256
ROPE_HALF = ATTN_DH // 4
LOG2E = 1.4426950408889634
QK_SCALE = ATTN_DH ** -0.5 * LOG2E
GLA_SCALE = GLA_DK ** -0.5

TOKEN_TILE = 512
FF_CHUNKS = ((0, 1536), (1536, D_FF))
GLA_TILE = 1024
CUM_TILE = 256
ATTN_Q_TILE = 1024
ATTN_UNIT_ROWS = 256
ATTN_UNIT_HEADS = 4
ATTN_KEY_TILE = 256
VT_ROWS = ATTN_DH + 16
VMEM_LIMIT = 56 * 1024 * 1024


def _const_spec(shape):
    return pl.BlockSpec(shape, lambda *_: (0,) * len(shape), pipeline_mode=pl.Buffered(1))


def _dot(a, b):
    return jnp.dot(a, b, preferred_element_type=F32)


def _dot_nt(a, b):
    return lax.dot_general(a, b, (((1,), (1,)), ((), ())), preferred_element_type=F32)


def _split_bf16(x):
    hi = x.astype(BF16)
    lo = (x - hi.astype(F32)).astype(BF16)
    return hi, lo


def _layer_norm(y, g, b):
    mu = jnp.mean(y, axis=-1, keepdims=True)
    d = y - mu
    var = jnp.mean(d * d, axis=-1, keepdims=True)
    return d * lax.rsqrt(var + LN_EPS) * g + b


def _swiglu(xb, wg_ref, wu_ref, wd_ref):
    out = None
    for lo, hi in FF_CHUNKS:
        g = _dot(xb, wg_ref[:, lo:hi])
        u = _dot(xb, wu_ref[:, lo:hi])
        hid = (g * jax.nn.sigmoid(g)) * u
        part = _dot(hid.astype(BF16), wd_ref[lo:hi, :])
        out = part if out is None else out + part
    return out


def _head_rms(x, avg, gain, eps):
    hi, lo = _split_bf16(x * x)
    w = min(x.shape[1], AVG_W)
    a = avg[:w, :w]
    ms = jnp.concatenate([_dot(hi[:, c:c + w], a) + _dot(lo[:, c:c + w], a) for c in range(0, x.shape[1], w)],
                         axis=1)
    return x * lax.rsqrt(ms + eps) * gain


def _rope(xs, c, sa, sb):
    return (xs * c + pltpu.roll(xs, LANES - ROPE_HALF, 1) * sa
            + pltpu.roll(xs, ROPE_HALF, 1) * sb)


def _kernel_a(x_ref, wg_ref, wu_ref, wd_ref, lng_ref, lnb_ref, win_ref, avg_ref, qg_ref, kg_ref,
              c_ref, sa_ref, sb_ref,
              h_ref, gqk_ref, gv_ref, gr_ref, z_ref, q_ref, k_ref, v_ref):
    x = x_ref[0]
    f = _swiglu(x.astype(BF16), wg_ref, wu_ref, wd_ref)
    h = _layer_norm(DEEPNORM_ALPHA * x + 0.5 * f, lng_ref[...], lnb_ref[...])
    h_ref[0] = h
    proj = _dot(h.astype(BF16), win_ref[...])
    gqk_ref[0] = proj[:, OFF_GQK:OFF_GV]
    gv_ref[0] = proj[:, OFF_GV:OFF_GR].astype(BF16)
    gr_ref[0] = proj[:, OFF_GR:OFF_AQ]
    z_ref[0] = proj[:, OFF_Z:D_IN]
    c, sa, sb = c_ref[...], sa_ref[...], sb_ref[...]
    avg = avg_ref[...]
    qn = _head_rms(proj[:, OFF_AQ:OFF_AK], avg, qg_ref[...], QK_EPS)
    for j in range(ATTN_Q_W // LANES):
        sl = slice(LANES * j, LANES * (j + 1))
        qr = _rope(qn[:, sl], c, sa, sb) * QK_SCALE
        for t in range(qr.shape[0] // ATTN_UNIT_ROWS):
            q_ref[0, t, sl, :] = qr[ATTN_UNIT_ROWS * t:ATTN_UNIT_ROWS * (t + 1), :].T.astype(BF16)
    kn = _head_rms(proj[:, OFF_AK:OFF_AV], avg, kg_ref[...], QK_EPS)
    kr = _rope(kn, c, sa, sb).astype(BF16)
    vt = proj[:, OFF_AV:OFF_Z].T.astype(BF16)
    ones = jnp.ones((VT_ROWS - ATTN_DH, vt.shape[1]), BF16)
    for g in range(ATTN_KV_HEADS):
        k_ref[0, g] = kr[:, ATTN_DH * g:ATTN_DH * (g + 1)]
        v_ref[0, g, 0:ATTN_DH, :] = vt[ATTN_DH * g:ATTN_DH * (g + 1), :]
        v_ref[0, g, ATTN_DH:VT_ROWS, :] = ones


def _call_a(x, w, tabs):
    bsz, n, _ = x.shape
    tm = TOKEN_TILE
    tok = lambda width: pl.BlockSpec((1, tm, width), lambda b, i: (b, i, 0))
    tab = pl.BlockSpec((tm, LANES), lambda b, i: (i, 0))
    kv_spec = pl.BlockSpec((1, ATTN_KV_HEADS, tm, ATTN_DH), lambda b, i: (b, 0, i, 0))
    vt_spec = pl.BlockSpec((1, ATTN_KV_HEADS, VT_ROWS, tm), lambda b, i: (b, 0, 0, i))
    qt_spec = pl.BlockSpec((1, tm // ATTN_UNIT_ROWS, ATTN_Q_W, ATTN_UNIT_ROWS), lambda b, i: (b, i, 0, 0))
    out_shape = (
        jax.ShapeDtypeStruct((bsz, n, D_MODEL), F32),
        jax.ShapeDtypeStruct((bsz, n, 2 * GLA_QK_W), F32),
        jax.ShapeDtypeStruct((bsz, n, GLA_V_W), BF16),
        jax.ShapeDtypeStruct((bsz, n, GLA_V_W), F32),
        jax.ShapeDtypeStruct((bsz, n, Z_W), F32),
        jax.ShapeDtypeStruct((bsz, n // ATTN_UNIT_ROWS, ATTN_Q_W, ATTN_UNIT_ROWS), BF16),
        jax.ShapeDtypeStruct((bsz, ATTN_KV_HEADS, n, ATTN_DH), BF16),
        jax.ShapeDtypeStruct((bsz, ATTN_KV_HEADS, VT_ROWS, n), BF16),
    )
    return pl.pallas_call(
        _kernel_a,
        grid=(bsz, n // tm),
        in_specs=[tok(D_MODEL),
                  _const_spec((D_MODEL, D_FF)), _const_spec((D_MODEL, D_FF)), _const_spec((D_FF, D_MODEL)),
                  _const_spec((1, D_MODEL)), _const_spec((1, D_MODEL)),
                  _const_spec((D_MODEL, D_IN)), _const_spec((AVG_W, AVG_W)),
                  _const_spec((1, ATTN_Q_W)), _const_spec((1, ATTN_KV_W)),
                  tab, tab, tab],
        out_specs=(tok(D_MODEL), tok(2 * GLA_QK_W), tok(GLA_V_W), tok(GLA_V_W), tok(Z_W), qt_spec,
                   kv_spec, vt_spec),
        out_shape=out_shape,
        compiler_params=pltpu.CompilerParams(dimension_semantics=("arbitrary", "arbitrary"),
                                             vmem_limit_bytes=VMEM_LIMIT),
        name="ffn1_inproj",
    )(x, w["ffn1_wg"], w["ffn1_wu"], w["ffn1_wd"], w["ln1_g"], w["ln1_b"], w["w_in"], w["avg"],
      w["q_gain"], w["k_gain"], *tabs)


def _gla_log_decay_cumsum(z_ref, w2_ref, b2_ref, tri_ref):
    tn = z_ref.shape[1]
    pre = _dot(z_ref[0].astype(BF16), w2_ref[...]) + b2_ref[...]
    log_a = (jnp.minimum(pre, 0.0) - jnp.log(1.0 + jnp.exp(-jnp.abs(pre)))) * (1.0 / GLA_TAU)
    hi, lo = _split_bf16(log_a)
    tri = tri_ref[...]
    cum = jnp.concatenate(
        [_dot(tri, hi[r:r + CUM_TILE]) + _dot(tri, lo[r:r + CUM_TILE]) for r in range(0, tn, CUM_TILE)],
        axis=0)
    return cum * LOG2E


def _gla_chunk(gqk_ref, gv_ref, out_ref, cum, state, c, half_mask, keep, reverse):
    c_len = GLA_CHUNK
    r0 = c_len * c
    cc = cum[r0:r0 + c_len]
    mid = cc[c_len // 2:c_len // 2 + 1] if reverse else cc[c_len // 2 - 1:c_len // 2]
    last = cc[0:1] if reverse else cc[c_len - 1:c_len]
    qc = gqk_ref[0, r0:r0 + c_len, 0:GLA_QK_W] * GLA_SCALE
    kc = gqk_ref[0, r0:r0 + c_len, GLA_QK_W:2 * GLA_QK_W]
    vc = gv_ref[0, r0:r0 + c_len, :]
    rel = cc - mid
    q_in = qc * jnp.exp2(rel)
    k_in = (kc * jnp.exp2(-rel)).astype(BF16)
    k_up = kc * jnp.exp2(last - cc)
    q_st = qc * jnp.exp2(cc)
    state_b = state.astype(BF16)
    scores, o_state = [], []
    for col in range(GLA_QK_W // LANES):
        lanes = slice(LANES * col, LANES * (col + 1))
        stack = lambda x: jnp.concatenate([x[:, lanes] * m for m in half_mask], axis=0).astype(BF16)
        scores.append(_dot_nt(stack(q_in), k_in[:, lanes]))
        o_state.append(_dot(stack(q_st), state_b[lanes, :]))
    p = jnp.where(keep, jnp.concatenate(scores, axis=0), 0.0).astype(BF16)
    o_state = jnp.concatenate(o_state, axis=0)
    xt = jnp.concatenate([k_up, jnp.broadcast_to(last, (c_len, GLA_QK_W))], axis=0).T
    k_up_t = xt[:, 0:c_len].astype(BF16)
    decay_col = jnp.exp2(xt[:, c_len:c_len + 1])
    upd = []
    for h in range(GLA_HEADS):
        rows = slice(c_len * h, c_len * (h + 1))
        vh = vc[:, GLA_DV * h:GLA_DV * (h + 1)]
        out_ref[0, r0:r0 + c_len, GLA_DV * h:GLA_DV * (h + 1)] = _dot(p[rows], vh) + o_state[rows]
        upd.append(_dot(k_up_t[rows], vh))
    return decay_col * state + jnp.concatenate(upd, axis=0)


def _kernel_gla(gqkf_ref, gvf_ref, zf_ref, gqkb_ref, gvb_ref, zb_ref,
                w2f_ref, b2f_ref, w2b_ref, b2b_ref, lower_ref, upper_ref,
                of_ref, ob_ref, state_ref):
    @pl.when(pl.program_id(1) == 0)
    def _():
        state_ref[...] = jnp.zeros_like(state_ref)

    c_len = GLA_CHUNK
    n_chunks = gqkf_ref.shape[1] // c_len
    cum_f = _gla_log_decay_cumsum(zf_ref, w2f_ref, b2f_ref, lower_ref)
    cum_b = _gla_log_decay_cumsum(zb_ref, w2b_ref, b2b_ref, upper_ref)
    first_half = lax.broadcasted_iota(jnp.int32, (c_len, LANES), 1) < GLA_DK
    half_mask = [first_half.astype(F32), 1.0 - first_half.astype(F32)]
    row = lax.broadcasted_iota(jnp.int32, (GLA_HEADS * c_len, c_len), 0) % c_len
    col = lax.broadcasted_iota(jnp.int32, (GLA_HEADS * c_len, c_len), 1)
    state_f, state_b = state_ref[0], state_ref[1]
    for c in range(n_chunks):
        state_f = _gla_chunk(gqkf_ref, gvf_ref, of_ref, cum_f, state_f, c, half_mask, col <= row, False)
        state_b = _gla_chunk(gqkb_ref, gvb_ref, ob_ref, cum_b, state_b, n_chunks - 1 - c, half_mask,
                             col > row, True)
    state_ref[0] = state_f
    state_ref[1] = state_b


def _call_gla(gqk, gv, z, w):
    bsz, n, _ = gqk.shape
    tn = GLA_TILE
    nb = n // tn
    fwd = lambda width: pl.BlockSpec((1, tn, width), lambda b, j: (b, j, 0))
    bwd = lambda width: pl.BlockSpec((1, tn, width), lambda b, j: (b, nb - 1 - j, 0))
    o_shape = jax.ShapeDtypeStruct((bsz, n, GLA_V_W), F32)
    return pl.pallas_call(
        _kernel_gla,
        grid=(bsz, nb),
        in_specs=[fwd(2 * GLA_QK_W), fwd(GLA_V_W), fwd(Z_W), bwd(2 * GLA_QK_W), bwd(GLA_V_W), bwd(Z_W),
                  _const_spec((Z_W, GLA_QK_W)), _const_spec((1, GLA_QK_W)),
                  _const_spec((Z_W, GLA_QK_W)), _const_spec((1, GLA_QK_W)),
                  _const_spec((CUM_TILE, CUM_TILE)), _const_spec((CUM_TILE, CUM_TILE))],
        out_specs=(fwd(GLA_V_W), bwd(GLA_V_W)),
        out_shape=(o_shape, o_shape),
        scratch_shapes=[pltpu.VMEM((2, GLA_HEADS * GLA_DK, GLA_DV), F32)],
        compiler_params=pltpu.CompilerParams(dimension_semantics=("arbitrary", "arbitrary"),
                                             vmem_limit_bytes=VMEM_LIMIT),
        name="gla_bidir",
    )(gqk, gv, z, gqk, gv, z, w["w2f"], w["b2f"], w["w2b"], w["b2b"], w["tri_lower"], w["tri_upper"])


def _kernel_attn(flag_ref, qt_ref, k_ref, vt_ref, o_ref, st_ref, m_ref):
    n = k_ref.shape[2]
    sub = ATTN_UNIT_ROWS
    hpu = ATTN_UNIT_HEADS
    width = hpu * sub
    groups = ATTN_HEADS // ATTN_KV_HEADS // hpu
    n_units = (ATTN_Q_TILE // sub) * groups
    tiles = [slice(t, t + ATTN_KEY_TILE) for t in range(0, n, ATTN_KEY_TILE)]
    step = pl.program_id(2)
    last_step = pl.num_programs(2) - 1

    def unit_window(at_step, u):
        s, hg = divmod(u, groups)
        return at_step * (ATTN_Q_TILE // sub) + s, hg

    def run_phase(u_scores, scores_step, u_pv):
        if u_scores is not None:
            tile_idx, hg = unit_window(scores_step, u_scores)
            heads = [qt_ref[0, tile_idx, ATTN_DH * h:ATTN_DH * (h + 1), :]
                     for h in range(hpu * hg, hpu * (hg + 1))]
            q_cur = jnp.concatenate(heads, axis=1)
            m_run = None
        if u_pv is not None:
            m_fin = m_ref[u_pv]
            acc = jnp.zeros((VT_ROWS, width), F32)
        for tile in tiles:
            if u_pv is not None:
                pt = jnp.exp2(st_ref[u_pv % 2, tile, :] - m_fin).astype(BF16)
                acc = acc + _dot(vt_ref[0, 0, :, tile], pt)
            if u_scores is not None:
                st = _dot(k_ref[0, 0, tile, :], q_cur)
                st_ref[u_scores % 2, tile, :] = st
                m_tile = jnp.max(st, axis=0, keepdims=True)
                m_run = m_tile if m_run is None else jnp.maximum(m_run, m_tile)
        if u_scores is not None:
            m_ref[u_scores] = m_run
        if u_pv is not None:
            tile_idx, hg = unit_window(step, u_pv)
            row0 = pl.multiple_of(tile_idx * sub, sub)
            o2 = (acc[0:ATTN_DH] / acc[ATTN_DH:ATTN_DH + 1]).T
            o_ref[0, pl.ds(row0, sub), ATTN_DH * hpu * hg:ATTN_DH * hpu * (hg + 1)] = (
                jnp.concatenate([o2[sub * i:sub * (i + 1)] for i in range(hpu)], axis=1).astype(BF16))

    pl.when(step == 0)(functools.partial(run_phase, 0, step, None))
    for u in range(n_units - 1):
        pl.when(flag_ref[u] == 0)(functools.partial(run_phase, u + 1, step, u))
    pl.when(step < last_step)(functools.partial(run_phase, 0, step + 1, n_units - 1))
    pl.when(step == last_step)(functools.partial(run_phase, None, None, n_units - 1))


def _call_attn(qt, k, vt):
    bsz, n = qt.shape[0], k.shape[2]
    group_w = ATTN_Q_W // ATTN_KV_HEADS
    width = ATTN_UNIT_HEADS * ATTN_UNIT_ROWS
    n_units = (ATTN_Q_TILE // ATTN_UNIT_ROWS) * (group_w // (ATTN_UNIT_HEADS * ATTN_DH))
    assert n_units % 2 == 0
    qt_spec = pl.BlockSpec((1, n // ATTN_UNIT_ROWS, group_w, ATTN_UNIT_ROWS), lambda b, g, i: (b, 0, g, 0))
    o_spec = pl.BlockSpec((1, n, group_w), lambda b, g, i: (b, 0, g))
    k_spec = pl.BlockSpec((1, 1, n, ATTN_DH), lambda b, g, i: (b, g, 0, 0))
    vt_spec = pl.BlockSpec((1, 1, VT_ROWS, n), lambda b, g, i: (b, g, 0, 0))
    return pl.pallas_call(
        _kernel_attn,
        grid=(bsz, ATTN_KV_HEADS, n // ATTN_Q_TILE),
        in_specs=[pl.BlockSpec(memory_space=pltpu.SMEM), qt_spec, k_spec, vt_spec],
        out_specs=o_spec,
        out_shape=jax.ShapeDtypeStruct((bsz, n, ATTN_Q_W), BF16),
        scratch_shapes=[pltpu.VMEM((2, n, width), F32),
                        pltpu.VMEM((n_units, 1, width), F32)],
        compiler_params=pltpu.CompilerParams(
            dimension_semantics=("arbitrary", "arbitrary", "arbitrary"),
            vmem_limit_bytes=VMEM_LIMIT),
        name="gqa_attention",
    )(jnp.zeros((n_units,), jnp.int32), qt, k, vt)


def _kernel_b(h1_ref, of_ref, ob_ref, gr_ref, oa_ref, p_ref, gn_ref, wog_ref, woa_ref,
              ln2g_ref, ln2b_ref, wg_ref, wu_ref, wd_ref, ln3g_ref, ln3b_ref, wpg_ref, bpg_ref, wpe_ref,
              out_ref):
    o = of_ref[0] + ob_ref[0]
    gn = gn_ref[...]
    normed = []
    for h in range(GLA_HEADS):
        oh = o[:, GLA_DV * h:GLA_DV * (h + 1)]
        ms = jnp.mean(oh * oh, axis=-1, keepdims=True)
        normed.append(oh * lax.rsqrt(ms + GN_EPS) * gn)
    gr = gr_ref[0]
    o_gla = jnp.concatenate(normed, axis=1) * (gr * jax.nn.sigmoid(gr))
    mix = _dot(o_gla.astype(BF16), wog_ref[...]) + _dot(oa_ref[0], woa_ref[...])
    h2 = _layer_norm(DEEPNORM_ALPHA * h1_ref[0] + mix, ln2g_ref[...], ln2b_ref[...])
    f = _swiglu(h2.astype(BF16), wg_ref, wu_ref, wd_ref)
    h3 = _layer_norm(DEEPNORM_ALPHA * h2 + 0.5 * f, ln3g_ref[...], ln3b_ref[...])
    gate = jax.nn.sigmoid(_dot(h3.astype(BF16), wpg_ref[...]) + bpg_ref[...])
    out_ref[0] = h3 + gate * _dot(p_ref[0].astype(BF16), wpe_ref[...])


def _call_b(h1, o_f, o_b, gr, o_att, p, w):
    bsz, n, _ = h1.shape
    tm = TOKEN_TILE
    tok = lambda width: pl.BlockSpec((1, tm, width), lambda b, i: (b, i, 0))
    return pl.pallas_call(
        _kernel_b,
        grid=(bsz, n // tm),
        in_specs=[tok(D_MODEL), tok(GLA_V_W), tok(GLA_V_W), tok(GLA_V_W), tok(ATTN_Q_W), tok(P_DIM),
                  _const_spec((1, GLA_DV)),
                  _const_spec((GLA_V_W, D_MODEL)), _const_spec((ATTN_Q_W, D_MODEL)),
                  _const_spec((1, D_MODEL)), _const_spec((1, D_MODEL)),
                  _const_spec((D_MODEL, D_FF)), _const_spec((D_MODEL, D_FF)), _const_spec((D_FF, D_MODEL)),
                  _const_spec((1, D_MODEL)), _const_spec((1, D_MODEL)),
                  _const_spec((D_MODEL, D_MODEL)), _const_spec((1, D_MODEL)), _const_spec((P_DIM, D_MODEL))],
        out_specs=tok(D_MODEL),
        out_shape=jax.ShapeDtypeStruct((bsz, n, D_MODEL), F32),
        compiler_params=pltpu.CompilerParams(dimension_semantics=("arbitrary", "arbitrary"),
                                             vmem_limit_bytes=VMEM_LIMIT),
        name="outproj_ffn2_embed",
    )(h1, o_f, o_b, gr, o_att, p, w["gn_g"], w["w_out_gla"], w["w_out_att"], w["ln2_g"], w["ln2_b"],
      w["ffn2_wg"], w["ffn2_wu"], w["ffn2_wd"], w["ln3_g"], w["ln3_b"], w["w_pg"], w["b_pg"], w["w_pe"])


def _rope_tables(n):
    t = jnp.arange(n, dtype=jnp.int32)
    row = (t // GRID_W).astype(F32)
    col = (t % GRID_W).astype(F32)
    axis_dim = ATTN_DH // 2
    inv_freq = ROPE_THETA ** (-jnp.arange(0, axis_dim, 2, dtype=F32) / axis_dim)
    lane = jnp.arange(LANES, dtype=jnp.int32) % ATTN_DH
    freq = inv_freq[lane % ROPE_HALF]
    pos = jnp.where((lane // axis_dim)[None, :] == 0, row[:, None], col[:, None])
    ang = pos * freq[None, :]
    first_half = ((lane % axis_dim) // ROPE_HALF == 0)[None, :]
    cos, sin = jnp.cos(ang), jnp.sin(ang)
    return cos, jnp.where(first_half, -sin, 0.0), jnp.where(first_half, 0.0, sin)


def _prepare(ffn1_wg, ffn1_wu, ffn1_wd, ln1_g, ln1_b, w_in, gla_w2f, gla_b2f, gla_w2b, gla_b2b,
             gla_gn_g, q_norm_g, k_norm_g, w_out, ln2_g, ln2_b, ffn2_wg, ffn2_wu, ffn2_wd,
             ln3_g, ln3_b, w_pg, b_pg, w_pe):
    row = lambda v: v.reshape(1, -1).astype(F32)
    z0 = 2 * GLA_QK_W + 2 * GLA_V_W
    w_in_r = jnp.concatenate([w_in[:, :z0], w_in[:, z0 + Z_W:], w_in[:, z0:z0 + Z_W]], axis=1)
    zeros = jnp.zeros((GLA_GATE_RANK, GLA_QK_W), F32)
    idx = jnp.arange(AVG_W)
    avg =jnp.where((idx[:, None] // ATTN_DH) == (idx[None, :] // ATTN_DH), 1.0 / ATTN_DH, 0.0)
    ci = jnp.arange(CUM_TILE)
    same_chunk = (ci[:, None] // GLA_CHUNK) == (ci[None, :] // GLA_CHUNK)
    lower = jnp.where(same_chunk & (ci[None, :] <= ci[:, None]), 1.0, 0.0)
    return {
        "ffn1_wg": ffn1_wg.astype(BF16), "ffn1_wu": ffn1_wu.astype(BF16), "ffn1_wd": ffn1_wd.astype(BF16),
        "ln1_g": row(ln1_g), "ln1_b": row(ln1_b),
        "w_in": w_in_r.astype(BF16), "avg": avg.astype(BF16),
        "q_gain": row(jnp.tile(q_norm_g, ATTN_HEADS)), "k_gain": row(jnp.tile(k_norm_g, ATTN_KV_HEADS)),
        "w2f": jnp.concatenate([gla_w2f, zeros], axis=0).astype(BF16), "b2f": row(gla_b2f),
        "w2b": jnp.concatenate([zeros, gla_w2b], axis=0).astype(BF16), "b2b": row(gla_b2b),
        "tri_lower": lower.astype(BF16), "tri_upper": lower.T.astype(BF16),
        "gn_g": row(gla_gn_g),
        "w_out_gla": w_out[:GLA_V_W].astype(BF16), "w_out_att": w_out[GLA_V_W:].astype(BF16),
        "ln2_g": row(ln2_g), "ln2_b": row(ln2_b),
        "ffn2_wg": ffn2_wg.astype(BF16), "ffn2_wu": ffn2_wu.astype(BF16), "ffn2_wd": ffn2_wd.astype(BF16),
        "ln3_g": row(ln3_g), "ln3_b": row(ln3_b),
        "w_pg": w_pg.astype(BF16), "b_pg": row(b_pg), "w_pe": w_pe.astype(BF16),
    }


def _encoder_layer(x, p, w):
    n = x.shape[1]
    assert n % GLA_TILE == 0 and n % ATTN_Q_TILE == 0 and n % TOKEN_TILE == 0 and n % GRID_W == 0
    h1, gqk, gv, gr, z, q, k, v = _call_a(x, w, _rope_tables(n))
    o_f, o_b = _call_gla(gqk, gv, z, w)
    o_att = _call_attn(q, k, v)
    return _call_b(h1, o_f, o_b, gr, o_att, p, w)


def kernel(x_prompt, x_sample, p_prompt, p_sample, ffn1_wg, ffn1_wu, ffn1_wd, ln1_g, ln1_b, w_in,
           gla_w2f, gla_b2f, gla_w2b, gla_b2b, gla_gn_g, q_norm_g, k_norm_g, w_out, ln2_g, ln2_b,
           ffn2_wg, ffn2_wu, ffn2_wd, ln3_g, ln3_b, w_pg, b_pg, w_pe):
    y_prompt, y_sample = x_prompt, x_sample
    for i in range(DEPTH):
        w = _prepare(ffn1_wg[i], ffn1_wu[i], ffn1_wd[i], ln1_g[i], ln1_b[i], w_in[i],
                     gla_w2f[i], gla_b2f[i], gla_w2b[i], gla_b2b[i], gla_gn_g[i], q_norm_g[i],
                     k_norm_g[i], w_out[i], ln2_g[i], ln2_b[i], ffn2_wg[i], ffn2_wu[i], ffn2_wd[i],
                     ln3_g[i], ln3_b[i], w_pg[i], b_pg[i], w_pe[i])
        y_prompt = _encoder_layer(y_prompt, p_prompt[i], w)
        y_sample = _encoder_layer(y_sample, p_sample[i], w)
    return (y_prompt, y_sample)
```

```python
import functools

import jax
import jax.numpy as jnp
from jax import lax
from jax.experimental import pallas as pl
from jax.experimental.pallas import tpu as pltpu

F32 = jnp.float32
BF16 = jnp.bfloat16

D_MODEL = 1024
D_FF = 2816
P_DIM = 256
GRID_W = 64
DEPTH = 1
GLA_HEADS = 4
GLA_DK = 64
GLA_DV = 128
GLA_GATE_RANK = 16
GLA_TAU = 16.0
GLA_CHUNK = 64
ATTN_HEADS = 8
ATTN_KV_HEADS = 2
ATTN_DH = 64
ROPE_THETA = 10000.0
LN_EPS = 1e-5
QK_EPS = 1e-6
GN_EPS = 1e-5
DEEPNORM_ALPHA = (2.0 * DEPTH) ** 0.25

GLA_QK_W = GLA_HEADS * GLA_DK
GLA_V_W = GLA_HEADS * GLA_DV
ATTN_Q_W = ATTN_HEADS * ATTN_DH
ATTN_KV_W = ATTN_KV_HEADS * ATTN_DH
Z_W = 2 * GLA_GATE_RANK
OFF_GQK = 0
OFF_GV = OFF_GQK + 2 * GLA_QK_W
OFF_GR = OFF_GV + GLA_V_W
OFF_AQ = OFF_GR + GLA_V_W
OFF_AK = OFF_AQ + ATTN_Q_W
OFF_AV = OFF_AK + ATTN_KV_W
OFF_Z = OFF_AV + ATTN_KV_W
D_IN = OFF_Z + Z_W

LANES = 128
AVG_W = 256
ROPE_HALF = ATTN_DH // 4
LOG2E = 1.4426950408889634
QK_SCALE = ATTN_DH ** -0.5 * LOG2E
GLA_SCALE = GLA_DK ** -0.5

TOKEN_TILE = 512
FF_CHUNKS = ((0, 1536), (1536, D_FF))
GLA_TILE = 1024
CUM_TILE = 256
ATTN_Q_TILE = 1024
ATTN_UNIT_ROWS = 256
ATTN_UNIT_HEADS = 4
ATTN_KEY_TILE = 256
VT_ROWS = ATTN_DH + 16
VMEM_LIMIT = 56 * 1024 * 1024


def _const_spec(shape):
    return pl.BlockSpec(shape, lambda *_: (0,) * len(shape), pipeline_mode=pl.Buffered(1))


def _dot(a, b):
    return jnp.dot(a, b, preferred_element_type=F32)


def _dot_nt(a, b):
    return lax.dot_general(a, b, (((1,), (1,)), ((), ())), preferred_element_type=F32)


def _split_bf16(x):
    hi = x.astype(BF16)
    lo = (x - hi.astype(F32)).astype(BF16)
    return hi, lo


def _layer_norm(y, g, b):
    mu = jnp.mean(y, axis=-1, keepdims=True)
    d = y - mu
    var = jnp.mean(d * d, axis=-1, keepdims=True)
    return d * lax.rsqrt(var + LN_EPS) * g + b


def _swiglu(xb, wg_ref, wu_ref, wd_ref):
    out = None
    for lo, hi in FF_CHUNKS:
        g = _dot(xb, wg_ref[:, lo:hi])
        u = _dot(xb, wu_ref[:, lo:hi])
        hid = (g * jax.nn.sigmoid(g)) * u
        part = _dot(hid.astype(BF16), wd_ref[lo:hi, :])
        out = part if out is None else out + part
    return out


def _head_rms(x, avg, gain, eps):
    hi, lo = _split_bf16(x * x)
    w = min(x.shape[1], AVG_W)
    a = avg[:w, :w]
    ms = jnp.concatenate([_dot(hi[:, c:c + w], a) + _dot(lo[:, c:c + w], a) for c in range(0, x.shape[1], w)],
                         axis=1)
    return x * lax.rsqrt(ms + eps) * gain


def _rope(xs, c, sa, sb):
    return (xs * c + pltpu.roll(xs, LANES - ROPE_HALF, 1) * sa
            + pltpu.roll(xs, ROPE_HALF, 1) * sb)


def _attn_operands(att_ref, avg_ref, qg_ref, kg_ref, c_ref, sa_ref, sb_ref, q_ref, k_ref, v_ref):
    att = att_ref[...]
    c, sa, sb = c_ref[...], sa_ref[...], sb_ref[...]
    avg = avg_ref[...]
    qn = _head_rms(att[:, 0:ATTN_Q_W], avg, qg_ref[...], QK_EPS)
    for j in range(ATTN_Q_W // LANES):
        sl = slice(LANES * j, LANES * (j + 1))
        qr = _rope(qn[:, sl], c, sa, sb) * QK_SCALE
        for t in range(qr.shape[0] // ATTN_UNIT_ROWS):
            q_ref[0, t, sl, :] = qr[ATTN_UNIT_ROWS * t:ATTN_UNIT_ROWS * (t + 1), :].T.astype(BF16)
    kn = _head_rms(att[:, ATTN_Q_W:ATTN_Q_W + ATTN_KV_W], avg, kg_ref[...], QK_EPS)
    kr = _rope(kn, c, sa, sb).astype(BF16)
    vt = att[:, ATTN_Q_W + ATTN_KV_W:].T.astype(BF16)
    ones = jnp.ones((VT_ROWS - ATTN_DH, vt.shape[1]), BF16)
    for g in range(ATTN_KV_HEADS):
        k_ref[0, g] = kr[:, ATTN_DH * g:ATTN_DH * (g + 1)]
        v_ref[0, g, 0:ATTN_DH, :] = vt[ATTN_DH * g:ATTN_DH * (g + 1), :]
        v_ref[0, g, ATTN_DH:VT_ROWS, :] = ones


def _kernel_a(x_ref, wg_ref, wu_ref, wd_ref, lng_ref, lnb_ref, win_ref, avg_ref, qg_ref, kg_ref,
              c_ref, sa_ref, sb_ref,
              h_ref, gqk_ref, gv_ref, gr_ref, z_ref, q_ref, k_ref, v_ref, att_ref):
    i = pl.program_id(1)
    n_tiles = pl.num_programs(1) - 1
    operands = functools.partial(_attn_operands, att_ref, avg_ref, qg_ref, kg_ref, c_ref, sa_ref, sb_ref,
                                 q_ref, k_ref, v_ref)

    @pl.when(i == 0)
    def _():
        att_ref[...] = jnp.zeros_like(att_ref)

    @pl.when(i < n_tiles)
    def _():
        operands()
        x = x_ref[0]
        f = _swiglu(x.astype(BF16), wg_ref, wu_ref, wd_ref)
        h = _layer_norm(DEEPNORM_ALPHA * x + 0.5 * f, lng_ref[...], lnb_ref[...])
        h_ref[0] = h
        proj = _dot(h.astype(BF16), win_ref[...])
        gqk_ref[0] = proj[:, OFF_GQK:OFF_GV]
        gv_ref[0] = proj[:, OFF_GV:OFF_GR].astype(BF16)
        gr_ref[0] = proj[:, OFF_GR:OFF_AQ]
        z_ref[0] = proj[:, OFF_Z:D_IN]
        att_ref[...] = proj[:, OFF_AQ:OFF_Z]

    pl.when(i == n_tiles)(operands)


def _call_a(x, w, tabs):
    bsz, n, _ = x.shape
    tm = TOKEN_TILE
    nt = n // tm
    cur = lambda i: jnp.minimum(i, nt - 1)
    prev = lambda i: jnp.maximum(i - 1, 0)
    tok = lambda width: pl.BlockSpec((1, tm, width), lambda b, i: (b, cur(i), 0))
    tab = pl.BlockSpec((tm, LANES), lambda b, i: (prev(i), 0))
    kv_spec = pl.BlockSpec((1, ATTN_KV_HEADS, tm, ATTN_DH), lambda b, i: (b, 0, prev(i), 0))
    vt_spec = pl.BlockSpec((1, ATTN_KV_HEADS, VT_ROWS, tm), lambda b, i: (b, 0, 0, prev(i)))
    qt_spec = pl.BlockSpec((1, tm // ATTN_UNIT_ROWS, ATTN_Q_W, ATTN_UNIT_ROWS), lambda b, i: (b, prev(i), 0, 0))
    out_shape = (
        jax.ShapeDtypeStruct((bsz, n, D_MODEL), F32),
        jax.ShapeDtypeStruct((bsz, n, 2 * GLA_QK_W), F32),
        jax.ShapeDtypeStruct((bsz, n, GLA_V_W), BF16),
        jax.ShapeDtypeStruct((bsz, n, GLA_V_W), F32),
        jax.ShapeDtypeStruct((bsz, n, Z_W), F32),
        jax.ShapeDtypeStruct((bsz, n // ATTN_UNIT_ROWS, ATTN_Q_W, ATTN_UNIT_ROWS), BF16),
        jax.ShapeDtypeStruct((bsz, ATTN_KV_HEADS, n, ATTN_DH), BF16),
        jax.ShapeDtypeStruct((bsz, ATTN_KV_HEADS, VT_ROWS, n), BF16),
    )
    return pl.pallas_call(
        _kernel_a,
        grid=(bsz, nt + 1),
        in_specs=[tok(D_MODEL),
                  _const_spec((D_MODEL, D_FF)), _const_spec((D_MODEL, D_FF)), _const_spec((D_FF, D_MODEL)),
                  _const_spec((1, D_MODEL)), _const_spec((1, D_MODEL)),
                  _const_spec((D_MODEL, D_IN)), _const_spec((AVG_W, AVG_W)),
                  _const_spec((1, ATTN_Q_W)), _const_spec((1, ATTN_KV_W)),
                  tab, tab, tab],
        out_specs=(tok(D_MODEL), tok(2 * GLA_QK_W), tok(GLA_V_W), tok(GLA_V_W), tok(Z_W), qt_spec,
                   kv_spec, vt_spec),
        out_shape=out_shape,
        scratch_shapes=[pltpu.VMEM((tm, OFF_Z - OFF_AQ), F32)],
        compiler_params=pltpu.CompilerParams(dimension_semantics=("arbitrary", "arbitrary"),
                                             vmem_limit_bytes=VMEM_LIMIT),
        name="ffn1_inproj",
    )(x, w["ffn1_wg"], w["ffn1_wu"], w["ffn1_wd"], w["ln1_g"], w["ln1_b"], w["w_in"], w["avg"],
      w["q_gain"], w["k_gain"], *tabs)


def _gla_log_decay_cumsum(z_ref, w2_ref, b2_ref, tri_ref):
    tn = z_ref.shape[1]
    pre = _dot(z_ref[0].astype(BF16), w2_ref[...]) + b2_ref[...]
    log_a = (jnp.minimum(pre, 0.0) - jnp.log(1.0 + jnp.exp(-jnp.abs(pre)))) * (1.0 / GLA_TAU)
    hi, lo = _split_bf16(log_a)
    tri = tri_ref[...]
    cum = jnp.concatenate(
        [_dot(tri, hi[r:r + CUM_TILE]) + _dot(tri, lo[r:r + CUM_TILE]) for r in range(0, tn, CUM_TILE)],
        axis=0)
    return cum * LOG2E


def _gla_chunk(gqk_ref, gv_ref, out_ref, cum, state, c, half_mask, keep, reverse):
    c_len = GLA_CHUNK
    r0 = c_len * c
    cc = cum[r0:r0 + c_len]
    mid = cc[c_len // 2:c_len // 2 + 1] if reverse else cc[c_len // 2 - 1:c_len // 2]
    last = cc[0:1] if reverse else cc[c_len - 1:c_len]
    qc = gqk_ref[0, r0:r0 + c_len, 0:GLA_QK_W] * GLA_SCALE
    kc = gqk_ref[0, r0:r0 + c_len, GLA_QK_W:2 * GLA_QK_W]
    vc = gv_ref[0, r0:r0 + c_len, :]
    rel = cc - mid
    q_in = qc * jnp.exp2(rel)
    k_in = (kc * jnp.exp2(-rel)).astype(BF16)
    k_up = kc * jnp.exp2(last - cc)
    q_st = qc * jnp.exp2(cc)
    state_b = state.astype(BF16)
    scores, o_state = [], []
    for col in range(GLA_QK_W // LANES):
        lanes = slice(LANES * col, LANES * (col + 1))
        stack = lambda x: jnp.concatenate([x[:, lanes] * m for m in half_mask], axis=0).astype(BF16)
        scores.append(_dot_nt(stack(q_in), k_in[:, lanes]))
        o_state.append(_dot(stack(q_st), state_b[lanes, :]))
    p = jnp.where(keep, jnp.concatenate(scores, axis=0), 0.0).astype(BF16)
    o_state = jnp.concatenate(o_state, axis=0)
    xt = jnp.concatenate([k_up, jnp.broadcast_to(last, (c_len, GLA_QK_W))], axis=0).T
    k_up_t = xt[:, 0:c_len].astype(BF16)
    decay_col = jnp.exp2(xt[:, c_len:c_len + 1])
    upd = []
    for h in range(GLA_HEADS):
        rows = slice(c_len * h, c_len * (h + 1))
        vh = vc[:, GLA_DV * h:GLA_DV * (h + 1)]
        out_ref[0, r0:r0 + c_len, GLA_DV * h:GLA_DV * (h + 1)] = _dot(p[rows], vh) + o_state[rows]
        upd.append(_dot(k_up_t[rows], vh))
    return decay_col * state + jnp.concatenate(upd, axis=0)


def _kernel_gla(gqkf_ref, gvf_ref, zf_ref, gqkb_ref, gvb_ref, zb_ref,
                w2f_ref, b2f_ref, w2b_ref, b2b_ref, lower_ref, upper_ref,
                of_ref, ob_ref, state_ref):
    @pl.when(pl.program_id(1) == 0)
    def _():
        state_ref[...] = jnp.zeros_like(state_ref)

    c_len = GLA_CHUNK
    n_chunks = gqkf_ref.shape[1] // c_len
    cum_f = _gla_log_decay_cumsum(zf_ref, w2f_ref, b2f_ref, lower_ref)
    cum_b = _gla_log_decay_cumsum(zb_ref, w2b_ref, b2b_ref, upper_ref)
    first_half = lax.broadcasted_iota(jnp.int32, (c_len, LANES), 1) < GLA_DK
    half_mask = [first_half.astype(F32), 1.0 - first_half.astype(F32)]
    row = lax.broadcasted_iota(jnp.int32, (GLA_HEADS * c_len, c_len), 0) % c_len
    col = lax.broadcasted_iota(jnp.int32, (GLA_HEADS * c_len, c_len), 1)
    state_f, state_b = state_ref[0], state_ref[1]
    for c in range(n_chunks):
        state_f = _gla_chunk(gqkf_ref, gvf_ref, of_ref, cum_f, state_f, c, half_mask, col <= row, False)
        state_b = _gla_chunk(gqkb_ref, gvb_ref, ob_ref, cum_b, state_b, n_chunks - 1 - c, half_mask,
                             col > row, True)
    state_ref[0] = state_f
    state_ref[1] = state_b


def _call_gla(gqk, gv, z, w):
    bsz, n, _ = gqk.shape
    tn = GLA_TILE
    nb = n // tn
    fwd = lambda width: pl.BlockSpec((1, tn, width), lambda b, j: (b, j, 0))
    bwd = lambda width: pl.BlockSpec((1, tn, width), lambda b, j: (b, nb - 1 - j, 0))
    o_shape = jax.ShapeDtypeStruct((bsz, n, GLA_V_W), F32)
    return pl.pallas_call(
        _kernel_gla,
        grid=(bsz, nb),
        in_specs=[fwd(2 * GLA_QK_W), fwd(GLA_V_W), fwd(Z_W), bwd(2 * GLA_QK_W), bwd(GLA_V_W), bwd(Z_W),
                  _const_spec((Z_W, GLA_QK_W)), _const_spec((1, GLA_QK_W)),
                  _const_spec((Z_W, GLA_QK_W)), _const_spec((1, GLA_QK_W)),
                  _const_spec((CUM_TILE, CUM_TILE)), _const_spec((CUM_TILE, CUM_TILE))],
        out_specs=(fwd(GLA_V_W), bwd(GLA_V_W)),
        out_shape=(o_shape, o_shape),
        scratch_shapes=[pltpu.VMEM((2, GLA_HEADS * GLA_DK, GLA_DV), F32)],
        compiler_params=pltpu.CompilerParams(dimension_semantics=("arbitrary", "arbitrary"),
                                             vmem_limit_bytes=VMEM_LIMIT),
        name="gla_bidir",
    )(gqk, gv, z, gqk, gv, z, w["w2f"], w["b2f"], w["w2b"], w["b2b"], w["tri_lower"], w["tri_upper"])


def _kernel_attn(flag_ref, qt_ref, k_ref, vt_ref, o_ref, st_ref, m_ref):
    n = k_ref.shape[2]
    sub = ATTN_UNIT_ROWS
    hpu = ATTN_UNIT_HEADS
    width = hpu * sub
    groups = ATTN_HEADS // ATTN_KV_HEADS // hpu
    n_units = (ATTN_Q_TILE // sub) * groups
    tiles = [slice(t, t + ATTN_KEY_TILE) for t in range(0, n, ATTN_KEY_TILE)]
    step = pl.program_id(2)
    last_step = pl.num_programs(2) - 1

    def unit_window(at_step, u):
        s, hg = divmod(u, groups)
        return at_step * (ATTN_Q_TILE // sub) + s, hg

    def run_phase(u_scores, scores_step, u_pv):
        if u_scores is not None:
            tile_idx, hg = unit_window(scores_step, u_scores)
            heads = [qt_ref[0, tile_idx, ATTN_DH * h:ATTN_DH * (h + 1), :]
                     for h in range(hpu * hg, hpu * (hg + 1))]
            q_cur = jnp.concatenate(heads, axis=1)
            m_run = None
        if u_pv is not None:
            m_fin = m_ref[u_pv]
            acc = jnp.zeros((VT_ROWS, width), F32)
        for tile in tiles:
            if u_pv is not None:
                pt = jnp.exp2(st_ref[u_pv % 2, tile, :] - m_fin).astype(BF16)
                acc = acc + _dot(vt_ref[0, 0, :, tile], pt)
            if u_scores is not None:
                st = _dot(k_ref[0, 0, tile, :], q_cur)
                st_ref[u_scores % 2, tile, :] = st
                m_tile = jnp.max(st, axis=0, keepdims=True)
                m_run = m_tile if m_run is None else jnp.maximum(m_run, m_tile)
        if u_scores is not None:
            m_ref[u_scores] = m_run
        if u_pv is not None:
            tile_idx, hg = unit_window(step, u_pv)
            row0 = pl.multiple_of(tile_idx * sub, sub)
            o2 = (acc[0:ATTN_DH] / acc[ATTN_DH:ATTN_DH + 1]).T
            o_ref[0, pl.ds(row0, sub), ATTN_DH * hpu * hg:ATTN_DH * hpu * (hg + 1)] = (
                jnp.concatenate([o2[sub * i:sub * (i + 1)] for i in range(hpu)], axis=1).astype(BF16))

    pl.when(step == 0)(functools.partial(run_phase, 0, step, None))
    for u in range(n_units - 1):
        pl.when(flag_ref[u] == 0)(functools.partial(run_phase, u + 1, step, u))
    pl.when(step < last_step)(functools.partial(run_phase, 0, step + 1, n_units - 1))
    pl.when(step == last_step)(functools.partial(run_phase, None, None, n_units - 1))


def _call_attn(qt, k, vt):
    bsz, n = qt.shape[0], k.shape[2]
    group_w = ATTN_Q_W // ATTN_KV_HEADS
    width = ATTN_UNIT_HEADS * ATTN_UNIT_ROWS
    n_units = (ATTN_Q_TILE // ATTN_UNIT_ROWS) * (group_w // (ATTN_UNIT_HEADS * ATTN_DH))
    assert n_units % 2 == 0
    qt_spec = pl.BlockSpec((1, n // ATTN_UNIT_ROWS, group_w, ATTN_UNIT_ROWS), lambda b, g, i: (b, 0, g, 0))
    o_spec = pl.BlockSpec((1, n, group_w), lambda b, g, i: (b, 0, g))
    k_spec = pl.BlockSpec((1, 1, n, ATTN_DH), lambda b, g, i: (b, g, 0, 0))
    vt_spec = pl.BlockSpec((1, 1, VT_ROWS, n), lambda b, g, i: (b, g, 0, 0))
    return pl.pallas_call(
        _kernel_attn,
        grid=(bsz, ATTN_KV_HEADS, n // ATTN_Q_TILE),
        in_specs=[pl.BlockSpec(memory_space=pltpu.SMEM), qt_spec, k_spec, vt_spec],
        out_specs=o_spec,
        out_shape=jax.ShapeDtypeStruct((bsz, n, ATTN_Q_W), BF16),
        scratch_shapes=[pltpu.VMEM((2, n, width), F32),
                        pltpu.VMEM((n_units, 1, width), F32)],
        compiler_params=pltpu.CompilerParams(
            dimension_semantics=("arbitrary", "arbitrary", "arbitrary"),
            vmem_limit_bytes=VMEM_LIMIT),
        name="gqa_attention",
    )(jnp.zeros((n_units,), jnp.int32), qt, k, vt)


def _kernel_b(h1_ref, of_ref, ob_ref, gr_ref, oa_ref, p_ref, gn_ref, wog_ref, woa_ref,
              ln2g_ref, ln2b_ref, wg_ref, wu_ref, wd_ref, ln3g_ref, ln3b_ref, wpg_ref, bpg_ref, wpe_ref,
              out_ref):
    o = of_ref[0] + ob_ref[0]
    gn = gn_ref[...]
    normed = []
    for h in range(GLA_HEADS):
        oh = o[:, GLA_DV * h:GLA_DV * (h + 1)]
        ms = jnp.mean(oh * oh, axis=-1, keepdims=True)
        normed.append(oh * lax.rsqrt(ms + GN_EPS) * gn)
    gr = gr_ref[0]
    o_gla = jnp.concatenate(normed, axis=1) * (gr * jax.nn.sigmoid(gr))
    mix = _dot(o_gla.astype(BF16), wog_ref[...]) + _dot(oa_ref[0], woa_ref[...])
    h2 = _layer_norm(DEEPNORM_ALPHA * h1_ref[0] + mix, ln2g_ref[...], ln2b_ref[...])
    f = _swiglu(h2.astype(BF16), wg_ref, wu_ref, wd_ref)
    h3 = _layer_norm(DEEPNORM_ALPHA * h2 + 0.5 * f, ln3g_ref[...], ln3b_ref[...])
    gate = jax.nn.sigmoid(_dot(h3.astype(BF16), wpg_ref[...]) + bpg_ref[...])
    out_ref[0] = h3 + gate * _dot(p_ref[0].astype(BF16), wpe_ref[...])


def _call_b(h1, o_f, o_b, gr, o_att, p, w):
    bsz, n, _ = h1.shape
    tm = TOKEN_TILE
    tok = lambda width: pl.BlockSpec((1, tm, width), lambda b, i: (b, i, 0))
    return pl.pallas_call(
        _kernel_b,
        grid=(bsz, n // tm),
        in_specs=[tok(D_MODEL), tok(GLA_V_W), tok(GLA_V_W), tok(GLA_V_W), tok(ATTN_Q_W), tok(P_DIM),
                  _const_spec((1, GLA_DV)),
                  _const_spec((GLA_V_W, D_MODEL)), _const_spec((ATTN_Q_W, D_MODEL)),
                  _const_spec((1, D_MODEL)), _const_spec((1, D_MODEL)),
                  _const_spec((D_MODEL, D_FF)), _const_spec((D_MODEL, D_FF)), _const_spec((D_FF, D_MODEL)),
                  _const_spec((1, D_MODEL)), _const_spec((1, D_MODEL)),
                  _const_spec((D_MODEL, D_MODEL)), _const_spec((1, D_MODEL)), _const_spec((P_DIM, D_MODEL))],
        out_specs=tok(D_MODEL),
        out_shape=jax.ShapeDtypeStruct((bsz, n, D_MODEL), F32),
        compiler_params=pltpu.CompilerParams(dimension_semantics=("arbitrary", "arbitrary"),
                                             vmem_limit_bytes=VMEM_LIMIT),
        name="outproj_ffn2_embed",
    )(h1, o_f, o_b, gr, o_att, p, w["gn_g"], w["w_out_gla"], w["w_out_att"], w["ln2_g"], w["ln2_b"],
      w["ffn2_wg"], w["ffn2_wu"], w["ffn2_wd"], w["ln3_g"], w["ln3_b"], w["w_pg"], w["b_pg"], w["w_pe"])


def _rope_tables(n):
    t = jnp.arange(n, dtype=jnp.int32)
    row = (t // GRID_W).astype(F32)
    col = (t % GRID_W).astype(F32)
    axis_dim = ATTN_DH // 2
    inv_freq = ROPE_THETA ** (-jnp.arange(0, axis_dim, 2, dtype=F32) / axis_dim)
    lane = jnp.arange(LANES, dtype=jnp.int32) % ATTN_DH
    freq = inv_freq[lane % ROPE_HALF]
    pos = jnp.where((lane // axis_dim)[None, :] == 0, row[:, None], col[:, None])
    ang = pos * freq[None, :]
    first_half = ((lane % axis_dim) // ROPE_HALF == 0)[None, :]
    cos, sin = jnp.cos(ang), jnp.sin(ang)
    return cos, jnp.where(first_half, -sin, 0.0), jnp.where(first_half, 0.0, sin)


def _prepare(ffn1_wg, ffn1_wu, ffn1_wd, ln1_g, ln1_b, w_in, gla_w2f, gla_b2f, gla_w2b, gla_b2b,
             gla_gn_g, q_norm_g, k_norm_g, w_out, ln2_g, ln2_b, ffn2_wg, ffn2_wu, ffn2_wd,
             ln3_g, ln3_b, w_pg, b_pg, w_pe):
    row = lambda v: v.reshape(1, -1).astype(F32)
    z0 = 2 * GLA_QK_W + 2 * GLA_V_W
    w_in_r = jnp.concatenate([w_in[:, :z0], w_in[:, z0 + Z_W:], w_in[:, z0:z0 + Z_W]], axis=1)
    zeros = jnp.zeros((GLA_GATE_RANK, GLA_QK_W), F32)
    idx = jnp.arange(AVG_W)
    avg =jnp.where((idx[:, None] // ATTN_DH) == (idx[None, :] // ATTN_DH), 1.0 / ATTN_DH, 0.0)
    ci = jnp.arange(CUM_TILE)
    same_chunk = (ci[:, None] // GLA_CHUNK) == (ci[None, :] // GLA_CHUNK)
    lower = jnp.where(same_chunk & (ci[None, :] <= ci[:, None]), 1.0, 0.0)
    return {
        "ffn1_wg": ffn1_wg.astype(BF16), "ffn1_wu": ffn1_wu.astype(BF16), "ffn1_wd": ffn1_wd.astype(BF16),
        "ln1_g": row(ln1_g), "ln1_b": row(ln1_b),
        "w_in": w_in_r.astype(BF16), "avg": avg.astype(BF16),
        "q_gain": row(jnp.tile(q_norm_g, ATTN_HEADS)), "k_gain": row(jnp.tile(k_norm_g, ATTN_KV_HEADS)),
        "w2f": jnp.concatenate([gla_w2f, zeros], axis=0).astype(BF16), "b2f": row(gla_b2f),
        "w2b": jnp.concatenate([zeros, gla_w2b], axis=0).astype(BF16), "b2b": row(gla_b2b),
        "tri_lower": lower.astype(BF16), "tri_upper": lower.T.astype(BF16),
        "gn_g": row(gla_gn_g),
        "w_out_gla": w_out[:GLA_V_W].astype(BF16), "w_out_att": w_out[GLA_V_W:].astype(BF16),
        "ln2_g": row(ln2_g), "ln2_b": row(ln2_b),
        "ffn2_wg": ffn2_wg.astype(BF16), "ffn2_wu": ffn2_wu.astype(BF16), "ffn2_wd": ffn2_wd.astype(BF16),
        "ln3_g": row(ln3_g), "ln3_b": row(ln3_b),
        "w_pg": w_pg.astype(BF16), "b_pg": row(b_pg), "w_pe": w_pe.astype(BF16),
    }


def _encoder_layer(x, p, w):
    n = x.shape[1]
    assert n % GLA_TILE == 0 and n % ATTN_Q_TILE == 0 and n % TOKEN_TILE == 0 and n % GRID_W == 0
    h1, gqk, gv, gr, z, q, k, v = _call_a(x, w, _rope_tables(n))
    o_f, o_b = _call_gla(gqk, gv, z, w)
    o_att = _call_attn(q, k, v)
    return _call_b(h1, o_f, o_b, gr, o_att, p, w)


def kernel(x_prompt, x_sample, p_prompt, p_sample, ffn1_wg, ffn1_wu, ffn1_wd, ln1_g, ln1_b, w_in,
           gla_w2f, gla_b2f, gla_w2b, gla_b2b, gla_gn_g, q_norm_g, k_norm_g, w_out, ln2_g, ln2_b,
           ffn2_wg, ffn2_wu, ffn2_wd, ln3_g, ln3_b, w_pg, b_pg, w_pe):
    y_prompt, y_sample = x_prompt, x_sample
    for i in range(DEPTH):
        w = _prepare(ffn1_wg[i], ffn1_wu[i], ffn1_wd[i], ln1_g[i], ln1_b[i], w_in[i],
                     gla_w2f[i], gla_b2f[i], gla_w2b[i], gla_b2b[i], gla_gn_g[i], q_norm_g[i],
                     k_norm_g[i], w_out[i], ln2_g[i], ln2_b[i], ffn2_wg[i], ffn2_wu[i], ffn2_wd[i],
                     ln3_g[i], ln3_b[i], w_pg[i], b_pg[i], w_pe[i])
        y_prompt = _encoder_layer(y_prompt, p_prompt[i], w)
        y_sample = _encoder_layer(y_sample, p_sample[i], w)
    return (y_prompt, y_sample)
```

```python
import functools

import jax
import jax.numpy as jnp
from jax import lax
from jax.experimental import pallas as pl
from jax.experimental.pallas import tpu as pltpu

F32 = jnp.float32
BF16 = jnp.bfloat16

D_MODEL = 1024
D_FF = 2816
P_DIM = 256
GRID_W = 64
DEPTH = 1
GLA_HEADS = 4
GLA_DK = 64
GLA_DV = 128
GLA_GATE_RANK = 16
GLA_TAU = 16.0
GLA_CHUNK = 64
ATTN_HEADS = 8
ATTN_KV_HEADS = 2
ATTN_DH = 64
ROPE_THETA = 10000.0
LN_EPS = 1e-5
QK_EPS = 1e-6
GN_EPS = 1e-5
DEEPNORM_ALPHA = (2.0 * DEPTH) ** 0.25

GLA_QK_W = GLA_HEADS * GLA_DK
GLA_V_W = GLA_HEADS * GLA_DV
ATTN_Q_W = ATTN_HEADS * ATTN_DH
ATTN_KV_W = ATTN_KV_HEADS * ATTN_DH
Z_W = 2 * GLA_GATE_RANK
OFF_GQK = 0
OFF_GV = OFF_GQK + 2 * GLA_QK_W
OFF_GR = OFF_GV + GLA_V_W
OFF_AQ = OFF_GR + GLA_V_W
OFF_AK = OFF_AQ + ATTN_Q_W
OFF_AV = OFF_AK + ATTN_KV_W
OFF_Z = OFF_AV + ATTN_KV_W
D_IN = OFF_Z + Z_W

LANES = 128
AVG_W = 256
ROPE_HALF = ATTN_DH // 4
LOG2E = 1.4426950408889634
QK_SCALE = ATTN_DH ** -0.5 * LOG2E
GLA_SCALE = GLA_DK ** -0.5

TOKEN_TILE = 512
FF_CHUNKS = ((0, 1536), (1536, D_FF))
GLA_TILE = 1024
CUM_TILE = 256
ATTN_Q_TILE = 1024
ATTN_UNIT_ROWS = 256
ATTN_UNIT_HEADS = 4
ATTN_KEY_TILE = 256
VT_ROWS = ATTN_DH + 16
VMEM_LIMIT = 56 * 1024 * 1024


def _const_spec(shape):
    return pl.BlockSpec(shape, lambda *_: (0,) * len(shape), pipeline_mode=pl.Buffered(1))


def _dot(a, b):
    return jnp.dot(a, b, preferred_element_type=F32)


def _dot_nt(a, b):
    return lax.dot_general(a, b, (((1,), (1,)), ((), ())), preferred_element_type=F32)


def _split_bf16(x):
    hi = x.astype(BF16)
    lo = (x - hi.astype(F32)).astype(BF16)
    return hi, lo


def _layer_norm(y, g, b):
    mu = jnp.mean(y, axis=-1, keepdims=True)
    d = y - mu
    var = jnp.mean(d * d, axis=-1, keepdims=True)
    return d * lax.rsqrt(var + LN_EPS) * g + b


def _swiglu(xb, wg_ref, wu_ref, wd_ref):
    out = None
    for lo, hi in FF_CHUNKS:
        g = _dot(xb, wg_ref[:, lo:hi])
        u = _dot(xb, wu_ref[:, lo:hi])
        hid = (g * jax.nn.sigmoid(g)) * u
        part = _dot(hid.astype(BF16), wd_ref[lo:hi, :])
        out = part if out is None else out + part
    return out


def _head_rms(x, avg, gain, eps):
    hi, lo = _split_bf16(x * x)
    w = min(x.shape[1], AVG_W)
    a = avg[:w, :w]
    ms = jnp.concatenate([_dot(hi[:, c:c + w], a) + _dot(lo[:, c:c + w], a) for c in range(0, x.shape[1], w)],
                         axis=1)
    return x * lax.rsqrt(ms + eps) * gain


def _rope(xs, c, sa, sb):
    return (xs * c + pltpu.roll(xs, LANES - ROPE_HALF, 1) * sa
            + pltpu.roll(xs, ROPE_HALF, 1) * sb)


def _attn_operands(att_ref, avg_ref, qg_ref, kg_ref, c_ref, sa_ref, sb_ref, q_ref, k_ref, v_ref):
    att = att_ref[...]
    c, sa, sb = c_ref[...], sa_ref[...], sb_ref[...]
    avg = avg_ref[...]
    qn = _head_rms(att[:, 0:ATTN_Q_W], avg, qg_ref[...], QK_EPS)
    for j in range(ATTN_Q_W // LANES):
        sl = slice(LANES * j, LANES * (j + 1))
        qr = _rope(qn[:, sl], c, sa, sb) * QK_SCALE
        for t in range(qr.shape[0] // ATTN_UNIT_ROWS):
            q_ref[0, t, sl, :] = qr[ATTN_UNIT_ROWS * t:ATTN_UNIT_ROWS * (t + 1), :].T.astype(BF16)
    kn = _head_rms(att[:, ATTN_Q_W:ATTN_Q_W + ATTN_KV_W], avg, kg_ref[...], QK_EPS)
    kr = _rope(kn, c, sa, sb).astype(BF16)
    vt = att[:, ATTN_Q_W + ATTN_KV_W:].T.astype(BF16)
    ones = jnp.ones((VT_ROWS - ATTN_DH, vt.shape[1]), BF16)
    for g in range(ATTN_KV_HEADS):
        k_ref[0, g] = kr[:, ATTN_DH * g:ATTN_DH * (g + 1)]
        v_ref[0, g, 0:ATTN_DH, :] = vt[ATTN_DH * g:ATTN_DH * (g + 1), :]
        v_ref[0, g, ATTN_DH:VT_ROWS, :] = ones


def _kernel_a(x_ref, wg_ref, wu_ref, wd_ref, lng_ref, lnb_ref, win_ref, avg_ref, qg_ref, kg_ref,
              c_ref, sa_ref, sb_ref,
              h_ref, gqk_ref, gv_ref, gr_ref, z_ref, q_ref, k_ref, v_ref, att_ref):
    i = pl.program_id(1)
    n_tiles = pl.num_programs(1) - 1
    operands = functools.partial(_attn_operands, att_ref, avg_ref, qg_ref, kg_ref, c_ref, sa_ref, sb_ref,
                                 q_ref, k_ref, v_ref)

    @pl.when(i == 0)
    def _():
        att_ref[...] = jnp.zeros_like(att_ref)

    @pl.when(i < n_tiles)
    def _():
        operands()
        x = x_ref[0]
        f = _swiglu(x.astype(BF16), wg_ref, wu_ref, wd_ref)
        h = _layer_norm(DEEPNORM_ALPHA * x + 0.5 * f, lng_ref[...], lnb_ref[...])
        h_ref[0] = h
        proj = _dot(h.astype(BF16), win_ref[...])
        gqk_ref[0] = proj[:, OFF_GQK:OFF_GV]
        gv_ref[0] = proj[:, OFF_GV:OFF_GR].astype(BF16)
        gr_ref[0] = proj[:, OFF_GR:OFF_AQ]
        z_ref[0] = proj[:, OFF_Z:D_IN]
        att_ref[...] = proj[:, OFF_AQ:OFF_Z]

    pl.when(i == n_tiles)(operands)


def _call_a(x, w, tabs):
    bsz, n, _ = x.shape
    tm = TOKEN_TILE
    nt = n // tm
    cur = lambda i: jnp.minimum(i, nt - 1)
    prev = lambda i: jnp.maximum(i - 1, 0)
    tok = lambda width: pl.BlockSpec((1, tm, width), lambda b, i: (b, cur(i), 0))
    tab = pl.BlockSpec((tm, LANES), lambda b, i: (prev(i), 0))
    kv_spec = pl.BlockSpec((1, ATTN_KV_HEADS, tm, ATTN_DH), lambda b, i: (b, 0, prev(i), 0))
    vt_spec = pl.BlockSpec((1, ATTN_KV_HEADS, VT_ROWS, tm), lambda b, i: (b, 0, 0, prev(i)))
    qt_spec = pl.BlockSpec((1, tm // ATTN_UNIT_ROWS, ATTN_Q_W, ATTN_UNIT_ROWS), lambda b, i: (b, prev(i), 0, 0))
    out_shape = (
        jax.ShapeDtypeStruct((bsz, n, D_MODEL), F32),
        jax.ShapeDtypeStruct((bsz, n, 2 * GLA_QK_W), F32),
        jax.ShapeDtypeStruct((bsz, n, GLA_V_W), BF16),
        jax.ShapeDtypeStruct((bsz, n, GLA_V_W), F32),
        jax.ShapeDtypeStruct((bsz, n, Z_W), F32),
        jax.ShapeDtypeStruct((bsz, n // ATTN_UNIT_ROWS, ATTN_Q_W, ATTN_UNIT_ROWS), BF16),
        jax.ShapeDtypeStruct((bsz, ATTN_KV_HEADS, n, ATTN_DH), BF16),
        jax.ShapeDtypeStruct((bsz, ATTN_KV_HEADS, VT_ROWS, n), BF16),
    )
    return pl.pallas_call(
        _kernel_a,
        grid=(bsz, nt + 1),
        in_specs=[tok(D_MODEL),
                  _const_spec((D_MODEL, D_FF)), _const_spec((D_MODEL, D_FF)), _const_spec((D_FF, D_MODEL)),
                  _const_spec((1, D_MODEL)), _const_spec((1, D_MODEL)),
                  _const_spec((D_MODEL, D_IN)), _const_spec((AVG_W, AVG_W)),
                  _const_spec((1, ATTN_Q_W)), _const_spec((1, ATTN_KV_W)),
                  tab, tab, tab],
        out_specs=(tok(D_MODEL), tok(2 * GLA_QK_W), tok(GLA_V_W), tok(GLA_V_W), tok(Z_W), qt_spec,
                   kv_spec, vt_spec),
        out_shape=out_shape,
        scratch_shapes=[pltpu.VMEM((tm, OFF_Z - OFF_AQ), F32)],
        compiler_params=pltpu.CompilerParams(dimension_semantics=("arbitrary", "arbitrary"),
                                             vmem_limit_bytes=VMEM_LIMIT),
        name="ffn1_inproj",
    )(x, w["ffn1_wg"], w["ffn1_wu"], w["ffn1_wd"], w["ln1_g"], w["ln1_b"], w["w_in"], w["avg"],
      w["q_gain"], w["k_gain"], *tabs)


def _gla_log_decay_cumsum(z_ref, w2_ref, b2_ref, tri_ref):
    tn = z_ref.shape[1]
    pre = _dot(z_ref[0].astype(BF16), w2_ref[...]) + b2_ref[...]
    log_a = (jnp.minimum(pre, 0.0) - jnp.log(1.0 + jnp.exp(-jnp.abs(pre)))) * (1.0 / GLA_TAU)
    hi, lo = _split_bf16(log_a)
    tri = tri_ref[...]
    cum = jnp.concatenate(
        [_dot(tri, hi[r:r + CUM_TILE]) + _dot(tri, lo[r:r + CUM_TILE]) for r in range(0, tn, CUM_TILE)],
        axis=0)
    return cum * LOG2E


def _gla_chunk(gqk_ref, gv_ref, out_ref, out_row0, accumulate, cum, state, c, half_mask, keep, reverse):
    c_len = GLA_CHUNK
    r0 = c_len * c
    cc = cum[r0:r0 + c_len]
    mid = cc[c_len // 2:c_len // 2 + 1] if reverse else cc[c_len // 2 - 1:c_len // 2]
    last = cc[0:1] if reverse else cc[c_len - 1:c_len]
    qc = gqk_ref[0, r0:r0 + c_len, 0:GLA_QK_W] * GLA_SCALE
    kc = gqk_ref[0, r0:r0 + c_len, GLA_QK_W:2 * GLA_QK_W]
    vc = gv_ref[0, r0:r0 + c_len, :]
    rel = cc - mid
    q_in = qc * jnp.exp2(rel)
    k_in = (kc * jnp.exp2(-rel)).astype(BF16)
    k_up = kc * jnp.exp2(last - cc)
    q_st = qc * jnp.exp2(cc)
    state_b = state.astype(BF16)
    scores, o_state = [], []
    for col in range(GLA_QK_W // LANES):
        lanes = slice(LANES * col, LANES * (col + 1))
        stack = lambda x: jnp.concatenate([x[:, lanes] * m for m in half_mask], axis=0).astype(BF16)
        scores.append(_dot_nt(stack(q_in), k_in[:, lanes]))
        o_state.append(_dot(stack(q_st), state_b[lanes, :]))
    p = jnp.where(keep, jnp.concatenate(scores, axis=0), 0.0).astype(BF16)
    o_state = jnp.concatenate(o_state, axis=0)
    xt = jnp.concatenate([k_up, jnp.broadcast_to(last, (c_len, GLA_QK_W))], axis=0).T
    k_up_t = xt[:, 0:c_len].astype(BF16)
    decay_col = jnp.exp2(xt[:, c_len:c_len + 1])
    upd = []
    for h in range(GLA_HEADS):
        rows = slice(c_len * h, c_len * (h + 1))
        vh = vc[:, GLA_DV * h:GLA_DV * (h + 1)]
        o_h = _dot(p[rows], vh) + o_state[rows]
        dst = (0, pl.ds(pl.multiple_of(out_row0 + r0, c_len), c_len), slice(GLA_DV * h, GLA_DV * (h + 1)))
        out_ref[dst] = out_ref[dst] + o_h if accumulate else o_h
        upd.append(_dot(k_up_t[rows], vh))
    return decay_col * state + jnp.concatenate(upd, axis=0)


def _kernel_gla(gqkf_ref, gvf_ref, zf_ref, gqkb_ref, gvb_ref, zb_ref,
                w2f_ref, b2f_ref, w2b_ref, b2b_ref, lower_ref, upper_ref,
                o_ref, state_ref):
    j = pl.program_id(1)
    nb = pl.num_programs(1)

    @pl.when(j == 0)
    def _():
        state_ref[...] = jnp.zeros_like(state_ref)

    tn = gqkf_ref.shape[1]
    pl.when(j < nb // 2)(functools.partial(
        _gla_step, gqkf_ref, gvf_ref, zf_ref, gqkb_ref, gvb_ref, zb_ref, w2f_ref, b2f_ref, w2b_ref, b2b_ref,
        lower_ref, upper_ref, o_ref, state_ref, j * tn, (nb - 1 - j) * tn, False))
    pl.when(j >= nb // 2)(functools.partial(
        _gla_step, gqkf_ref, gvf_ref, zf_ref, gqkb_ref, gvb_ref, zb_ref, w2f_ref, b2f_ref, w2b_ref, b2b_ref,
        lower_ref, upper_ref, o_ref, state_ref, j * tn, (nb - 1 - j) * tn, True))


def _gla_step(gqkf_ref, gvf_ref, zf_ref, gqkb_ref, gvb_ref, zb_ref, w2f_ref, b2f_ref, w2b_ref, b2b_ref,
              lower_ref, upper_ref, o_ref, state_ref, row_f, row_b, accumulate):
    c_len = GLA_CHUNK
    n_chunks = gqkf_ref.shape[1] // c_len
    cum_f = _gla_log_decay_cumsum(zf_ref, w2f_ref, b2f_ref, lower_ref)
    cum_b = _gla_log_decay_cumsum(zb_ref, w2b_ref, b2b_ref, upper_ref)
    first_half = lax.broadcasted_iota(jnp.int32, (c_len, LANES), 1) < GLA_DK
    half_mask = [first_half.astype(F32), 1.0 - first_half.astype(F32)]
    row = lax.broadcasted_iota(jnp.int32, (GLA_HEADS * c_len, c_len), 0) % c_len
    col = lax.broadcasted_iota(jnp.int32, (GLA_HEADS * c_len, c_len), 1)
    state_f, state_b = state_ref[0], state_ref[1]
    for c in range(n_chunks):
        state_f = _gla_chunk(gqkf_ref, gvf_ref, o_ref, row_f, accumulate, cum_f, state_f, c, half_mask,
                             col <= row, False)
        state_b = _gla_chunk(gqkb_ref, gvb_ref, o_ref, row_b, accumulate, cum_b, state_b, n_chunks - 1 - c,
                             half_mask, col > row, True)
    state_ref[0] = state_f
    state_ref[1] = state_b


def _call_gla(gqk, gv, z, w):
    bsz, n, _ = gqk.shape
    tn = GLA_TILE
    nb = n // tn
    fwd = lambda width: pl.BlockSpec((1, tn, width), lambda b, j: (b, j, 0))
    bwd = lambda width: pl.BlockSpec((1, tn, width), lambda b, j: (b, nb - 1 - j, 0))
    assert nb % 2 == 0
    return pl.pallas_call(
        _kernel_gla,
        grid=(bsz, nb),
        in_specs=[fwd(2 * GLA_QK_W), fwd(GLA_V_W), fwd(Z_W), bwd(2 * GLA_QK_W), bwd(GLA_V_W), bwd(Z_W),
                  _const_spec((Z_W, GLA_QK_W)), _const_spec((1, GLA_QK_W)),
                  _const_spec((Z_W, GLA_QK_W)), _const_spec((1, GLA_QK_W)),
                  _const_spec((CUM_TILE, CUM_TILE)), _const_spec((CUM_TILE, CUM_TILE))],
        out_specs=pl.BlockSpec((1, n, GLA_V_W), lambda b, j: (b, 0, 0)),
        out_shape=jax.ShapeDtypeStruct((bsz, n, GLA_V_W), F32),
        scratch_shapes=[pltpu.VMEM((2, GLA_HEADS * GLA_DK, GLA_DV), F32)],
        compiler_params=pltpu.CompilerParams(dimension_semantics=("arbitrary", "arbitrary"),
                                             vmem_limit_bytes=VMEM_LIMIT),
        name="gla_bidir",
    )(gqk, gv, z, gqk, gv, z, w["w2f"], w["b2f"], w["w2b"], w["b2b"], w["tri_lower"], w["tri_upper"])


def _kernel_attn(flag_ref, qt_ref, k_ref, vt_ref, o_ref, st_ref, m_ref):
    n = k_ref.shape[2]
    sub = ATTN_UNIT_ROWS
    hpu = ATTN_UNIT_HEADS
    width = hpu * sub
    groups = ATTN_HEADS // ATTN_KV_HEADS // hpu
    n_units = (ATTN_Q_TILE // sub) * groups
    tiles = [slice(t, t + ATTN_KEY_TILE) for t in range(0, n, ATTN_KEY_TILE)]
    step = pl.program_id(2)
    last_step = pl.num_programs(2) - 1

    def unit_window(at_step, u):
        s, hg = divmod(u, groups)
        return at_step * (ATTN_Q_TILE // sub) + s, hg

    def run_phase(u_scores, scores_step, u_pv):
        if u_scores is not None:
            tile_idx, hg = unit_window(scores_step, u_scores)
            heads = [qt_ref[0, tile_idx, ATTN_DH * h:ATTN_DH * (h + 1), :]
                     for h in range(hpu * hg, hpu * (hg + 1))]
            q_cur = jnp.concatenate(heads, axis=1)
            m_run = None
        if u_pv is not None:
            m_fin = m_ref[u_pv]
            acc = jnp.zeros((VT_ROWS, width), F32)
        for tile in tiles:
            if u_pv is not None:
                pt = jnp.exp2(st_ref[u_pv % 2, tile, :] - m_fin).astype(BF16)
                acc = acc + _dot(vt_ref[0, 0, :, tile], pt)
            if u_scores is not None:
                st = _dot(k_ref[0, 0, tile, :], q_cur)
                st_ref[u_scores % 2, tile, :] = st
                m_tile = jnp.max(st, axis=0, keepdims=True)
                m_run = m_tile if m_run is None else jnp.maximum(m_run, m_tile)
        if u_scores is not None:
            m_ref[u_scores] = m_run
        if u_pv is not None:
            tile_idx, hg = unit_window(step, u_pv)
            row0 = pl.multiple_of(tile_idx * sub, sub)
            o2 = (acc[0:ATTN_DH] / acc[ATTN_DH:ATTN_DH + 1]).T
            o_ref[0, pl.ds(row0, sub), ATTN_DH * hpu * hg:ATTN_DH * hpu * (hg + 1)] = (
                jnp.concatenate([o2[sub * i:sub * (i + 1)] for i in range(hpu)], axis=1).astype(BF16))

    pl.when(step == 0)(functools.partial(run_phase, 0, step, None))
    for u in range(n_units - 1):
        pl.when(flag_ref[u] == 0)(functools.partial(run_phase, u + 1, step, u))
    pl.when(step < last_step)(functools.partial(run_phase, 0, step + 1, n_units - 1))
    pl.when(step == last_step)(functools.partial(run_phase, None, None, n_units - 1))


def _call_attn(qt, k, vt):
    bsz, n = qt.shape[0], k.shape[2]
    group_w = ATTN_Q_W // ATTN_KV_HEADS
    width = ATTN_UNIT_HEADS * ATTN_UNIT_ROWS
    n_units = (ATTN_Q_TILE // ATTN_UNIT_ROWS) * (group_w // (ATTN_UNIT_HEADS * ATTN_DH))
    assert n_units % 2 == 0
    qt_spec = pl.BlockSpec((1, n // ATTN_UNIT_ROWS, group_w, ATTN_UNIT_ROWS), lambda b, g, i: (b, 0, g, 0))
    o_spec = pl.BlockSpec((1, n, group_w), lambda b, g, i: (b, 0, g))
    k_spec = pl.BlockSpec((1, 1, n, ATTN_DH), lambda b, g, i: (b, g, 0, 0))
    vt_spec = pl.BlockSpec((1, 1, VT_ROWS, n), lambda b, g, i: (b, g, 0, 0))
    return pl.pallas_call(
        _kernel_attn,
        grid=(bsz, ATTN_KV_HEADS, n // ATTN_Q_TILE),
        in_specs=[pl.BlockSpec(memory_space=pltpu.SMEM), qt_spec, k_spec, vt_spec],
        out_specs=o_spec,
        out_shape=jax.ShapeDtypeStruct((bsz, n, ATTN_Q_W), BF16),
        scratch_shapes=[pltpu.VMEM((2, n, width), F32),
                        pltpu.VMEM((n_units, 1, width), F32)],
        compiler_params=pltpu.CompilerParams(
            dimension_semantics=("arbitrary", "arbitrary", "arbitrary"),
            vmem_limit_bytes=VMEM_LIMIT),
        name="gqa_attention",
    )(jnp.zeros((n_units,), jnp.int32), qt, k, vt)


def _kernel_b(h1_ref, og_ref, gr_ref, oa_ref, p_ref, gn_ref, wog_ref, woa_ref,
              ln2g_ref, ln2b_ref, wg_ref, wu_ref, wd_ref, ln3g_ref, ln3b_ref, wpg_ref, bpg_ref, wpe_ref,
              out_ref):
    o = og_ref[0]
    gn = gn_ref[...]
    normed = []
    for h in range(GLA_HEADS):
        oh = o[:, GLA_DV * h:GLA_DV * (h + 1)]
        ms = jnp.mean(oh * oh, axis=-1, keepdims=True)
        normed.append(oh * lax.rsqrt(ms + GN_EPS) * gn)
    gr = gr_ref[0]
    o_gla = jnp.concatenate(normed, axis=1) * (gr * jax.nn.sigmoid(gr))
    mix = _dot(o_gla.astype(BF16), wog_ref[...]) + _dot(oa_ref[0], woa_ref[...])
    h2 = _layer_norm(DEEPNORM_ALPHA * h1_ref[0] + mix, ln2g_ref[...], ln2b_ref[...])
    f = _swiglu(h2.astype(BF16), wg_ref, wu_ref, wd_ref)
    h3 = _layer_norm(DEEPNORM_ALPHA * h2 + 0.5 * f, ln3g_ref[...], ln3b_ref[...])
    gate = jax.nn.sigmoid(_dot(h3.astype(BF16), wpg_ref[...]) + bpg_ref[...])
    out_ref[0] = h3 + gate * _dot(p_ref[0].astype(BF16), wpe_ref[...])


def _call_b(h1, o_gla, gr, o_att, p, w):
    bsz, n, _ = h1.shape
    tm = TOKEN_TILE
    tok = lambda width: pl.BlockSpec((1, tm, width), lambda b, i: (b, i, 0))
    return pl.pallas_call(
        _kernel_b,
        grid=(bsz, n // tm),
        in_specs=[tok(D_MODEL), tok(GLA_V_W), tok(GLA_V_W), tok(ATTN_Q_W), tok(P_DIM),
                  _const_spec((1, GLA_DV)),
                  _const_spec((GLA_V_W, D_MODEL)), _const_spec((ATTN_Q_W, D_MODEL)),
                  _const_spec((1, D_MODEL)), _const_spec((1, D_MODEL)),
                  _const_spec((D_MODEL, D_FF)), _const_spec((D_MODEL, D_FF)), _const_spec((D_FF, D_MODEL)),
                  _const_spec((1, D_MODEL)), _const_spec((1, D_MODEL)),
                  _const_spec((D_MODEL, D_MODEL)), _const_spec((1, D_MODEL)), _const_spec((P_DIM, D_MODEL))],
        out_specs=tok(D_MODEL),
        out_shape=jax.ShapeDtypeStruct((bsz, n, D_MODEL), F32),
        compiler_params=pltpu.CompilerParams(dimension_semantics=("arbitrary", "arbitrary"),
                                             vmem_limit_bytes=VMEM_LIMIT),
        name="outproj_ffn2_embed",
    )(h1, o_gla, gr, o_att, p, w["gn_g"], w["w_out_gla"], w["w_out_att"], w["ln2_g"], w["ln2_b"],
      w["ffn2_wg"], w["ffn2_wu"], w["ffn2_wd"], w["ln3_g"], w["ln3_b"], w["w_pg"], w["b_pg"], w["w_pe"])


def _rope_tables(n):
    t = jnp.arange(n, dtype=jnp.int32)
    row = (t // GRID_W).astype(F32)
    col = (t % GRID_W).astype(F32)
    axis_dim = ATTN_DH // 2
    inv_freq = ROPE_THETA ** (-jnp.arange(0, axis_dim, 2, dtype=F32) / axis_dim)
    lane = jnp.arange(LANES, dtype=jnp.int32) % ATTN_DH
    freq = inv_freq[lane % ROPE_HALF]
    pos = jnp.where((lane // axis_dim)[None, :] == 0, row[:, None], col[:, None])
    ang = pos * freq[None, :]
    first_half = ((lane % axis_dim) // ROPE_HALF == 0)[None, :]
    cos, sin = jnp.cos(ang), jnp.sin(ang)
    return cos, jnp.where(first_half, -sin, 0.0), jnp.where(first_half, 0.0, sin)


def _prepare(ffn1_wg, ffn1_wu, ffn1_wd, ln1_g, ln1_b, w_in, gla_w2f, gla_b2f, gla_w2b, gla_b2b,
             gla_gn_g, q_norm_g, k_norm_g, w_out, ln2_g, ln2_b, ffn2_wg, ffn2_wu, ffn2_wd,
             ln3_g, ln3_b, w_pg, b_pg, w_pe):
    row = lambda v: v.reshape(1, -1).astype(F32)
    z0 = 2 * GLA_QK_W + 2 * GLA_V_W
    w_in_r = jnp.concatenate([w_in[:, :z0], w_in[:, z0 + Z_W:], w_in[:, z0:z0 + Z_W]], axis=1)
    zeros = jnp.zeros((GLA_GATE_RANK, GLA_QK_W), F32)
    idx = jnp.arange(AVG_W)
    avg =jnp.where((idx[:, None] // ATTN_DH) == (idx[None, :] // ATTN_DH), 1.0 / ATTN_DH, 0.0)
    ci = jnp.arange(CUM_TILE)
    same_chunk = (ci[:, None] // GLA_CHUNK) == (ci[None, :] // GLA_CHUNK)
    lower = jnp.where(same_chunk & (ci[None, :] <= ci[:, None]), 1.0, 0.0)
    return {
        "ffn1_wg": ffn1_wg.astype(BF16), "ffn1_wu": ffn1_wu.astype(BF16), "ffn1_wd": ffn1_wd.astype(BF16),
        "ln1_g": row(ln1_g), "ln1_b": row(ln1_b),
        "w_in": w_in_r.astype(BF16), "avg": avg.astype(BF16),
        "q_gain": row(jnp.tile(q_norm_g, ATTN_HEADS)), "k_gain": row(jnp.tile(k_norm_g, ATTN_KV_HEADS)),
        "w2f": jnp.concatenate([gla_w2f, zeros], axis=0).astype(BF16), "b2f": row(gla_b2f),
        "w2b": jnp.concatenate([zeros, gla_w2b], axis=0).astype(BF16), "b2b": row(gla_b2b),
        "tri_lower": lower.astype(BF16), "tri_upper": lower.T.astype(BF16),
        "gn_g": row(gla_gn_g),
        "w_out_gla": w_out[:GLA_V_W].astype(BF16), "w_out_att": w_out[GLA_V_W:].astype(BF16),
        "ln2_g": row(ln2_g), "ln2_b": row(ln2_b),
        "ffn2_wg": ffn2_wg.astype(BF16), "ffn2_wu": ffn2_wu.astype(BF16), "ffn2_wd": ffn2_wd.astype(BF16),
        "ln3_g": row(ln3_g), "ln3_b": row(ln3_b),
        "w_pg": w_pg.astype(BF16), "b_pg": row(b_pg), "w_pe": w_pe.astype(BF16),
    }


def _encoder_layer(x, p, w):
    n = x.shape[1]
    assert n % GLA_TILE == 0 and n % ATTN_Q_TILE == 0 and n % TOKEN_TILE == 0 and n % GRID_W == 0
    h1, gqk, gv, gr, z, q, k, v = _call_a(x, w, _rope_tables(n))
    o_gla = _call_gla(gqk, gv, z, w)
    o_att = _call_attn(q, k, v)
    return _call_b(h1, o_gla, gr, o_att, p, w)


def kernel(x_prompt, x_sample, p_prompt, p_sample, ffn1_wg, ffn1_wu, ffn1_wd, ln1_g, ln1_b, w_in,
           gla_w2f, gla_b2f, gla_w2b, gla_b2b, gla_gn_g, q_norm_g, k_norm_g, w_out, ln2_g, ln2_b,
           ffn2_wg, ffn2_wu, ffn2_wd, ln3_g, ln3_b, w_pg, b_pg, w_pe):
    y_prompt, y_sample = x_prompt, x_sample
    for i in range(DEPTH):
        w = _prepare(ffn1_wg[i], ffn1_wu[i], ffn1_wd[i], ln1_g[i], ln1_b[i], w_in[i],
                     gla_w2f[i], gla_b2f[i], gla_w2b[i], gla_b2b[i], gla_gn_g[i], q_norm_g[i],
                     k_norm_g[i], w_out[i], ln2_g[i], ln2_b[i], ffn2_wg[i], ffn2_wu[i], ffn2_wd[i],
                     ln3_g[i], ln3_b[i], w_pg[i], b_pg[i], w_pe[i])
        y_prompt = _encoder_layer(y_prompt, p_prompt[i], w)
        y_sample = _encoder_layer(y_sample, p_sample[i], w)
    return (y_prompt, y_sample)
```

```python
import functools

import jax
import jax.numpy as jnp
from jax import lax
from jax.experimental import pallas as pl
from jax.experimental.pallas import tpu as pltpu

F32 = jnp.float32
BF16 = jnp.bfloat16

D_MODEL = 1024
D_FF = 2816
P_DIM = 256
GRID_W = 64
DEPTH = 1
GLA_HEADS = 4
GLA_DK = 64
GLA_DV = 128
GLA_GATE_RANK = 16
GLA_TAU = 16.0
GLA_CHUNK = 64
ATTN_HEADS = 8
ATTN_KV_HEADS = 2
ATTN_DH = 64
ROPE_THETA = 10000.0
LN_EPS = 1e-5
QK_EPS = 1e-6
GN_EPS = 1e-5
DEEPNORM_ALPHA = (2.0 * DEPTH) ** 0.25

GLA_QK_W = GLA_HEADS * GLA_DK
GLA_V_W = GLA_HEADS * GLA_DV
ATTN_Q_W = ATTN_HEADS * ATTN_DH
ATTN_KV_W = ATTN_KV_HEADS * ATTN_DH
Z_W = 2 * GLA_GATE_RANK
OFF_GQK = 0
OFF_GV = OFF_GQK + 2 * GLA_QK_W
OFF_GR = OFF_GV + GLA_V_W
OFF_AQ = OFF_GR + GLA_V_W
OFF_AK = OFF_AQ + ATTN_Q_W
OFF_AV = OFF_AK + ATTN_KV_W
OFF_Z = OFF_AV + ATTN_KV_W
D_IN = OFF_Z + Z_W

LANES = 128
AVG_W = 256
ROPE_HALF = ATTN_DH // 4
LOG2E = 1.4426950408889634
QK_SCALE = ATTN_DH ** -0.5 * LOG2E
GLA_SCALE = GLA_DK ** -0.5

TOKEN_TILE = 512
FF_CHUNKS = ((0, 1536), (1536, D_FF))
GLA_TILE = 1024
CUM_TILE = 256
ATTN_Q_TILE = 1024
ATTN_UNIT_ROWS = 256
ATTN_UNIT_HEADS = 4
ATTN_KEY_TILE = 256
VT_ROWS = ATTN_DH + 16
VMEM_LIMIT = 56 * 1024 * 1024


def _const_spec(shape):
    return pl.BlockSpec(shape, lambda *_: (0,) * len(shape), pipeline_mode=pl.Buffered(1))


def _dot(a, b):
    return jnp.dot(a, b, preferred_element_type=F32)


def _dot_nt(a, b):
    return lax.dot_general(a, b, (((1,), (1,)), ((), ())), preferred_element_type=F32)


def _split_bf16(x):
    hi = x.astype(BF16)
    lo = (x - hi.astype(F32)).astype(BF16)
    return hi, lo


def _layer_norm(y, g, b):
    mu = jnp.mean(y, axis=-1, keepdims=True)
    d = y - mu
    var = jnp.mean(d * d, axis=-1, keepdims=True)
    return d * lax.rsqrt(var + LN_EPS) * g + b


def _swiglu(xb, wg_ref, wu_ref, wd_ref):
    out = None
    for lo, hi in FF_CHUNKS:
        g = _dot(xb, wg_ref[:, lo:hi])
        u = _dot(xb, wu_ref[:, lo:hi])
        hid = (g * jax.nn.sigmoid(g)) * u
        part = _dot(hid.astype(BF16), wd_ref[lo:hi, :])
        out = part if out is None else out + part
    return out


def _head_rms(x, avg, gain, eps):
    hi, lo = _split_bf16(x * x)
    w = min(x.shape[1], AVG_W)
    a = avg[:w, :w]
    ms = jnp.concatenate([_dot(hi[:, c:c + w], a) + _dot(lo[:, c:c + w], a) for c in range(0, x.shape[1], w)],
                         axis=1)
    return x * lax.rsqrt(ms + eps) * gain


def _rope(xs, c, sa, sb):
    return (xs * c + pltpu.roll(xs, LANES - ROPE_HALF, 1) * sa
            + pltpu.roll(xs, ROPE_HALF, 1) * sb)


def _attn_operands(att_ref, avg_ref, qg_ref, kg_ref, c_ref, sa_ref, sb_ref, q_ref, k_ref, v_ref):
    att = att_ref[...]
    c, sa, sb = c_ref[...], sa_ref[...], sb_ref[...]
    avg = avg_ref[...]
    qn = _head_rms(att[:, 0:ATTN_Q_W], avg, qg_ref[...], QK_EPS)
    for j in range(ATTN_Q_W // LANES):
        sl = slice(LANES * j, LANES * (j + 1))
        qr = _rope(qn[:, sl], c, sa, sb) * QK_SCALE
        for t in range(qr.shape[0] // ATTN_UNIT_ROWS):
            q_ref[0, t, sl, :] = qr[ATTN_UNIT_ROWS * t:ATTN_UNIT_ROWS * (t + 1), :].T.astype(BF16)
    kn = _head_rms(att[:, ATTN_Q_W:ATTN_Q_W + ATTN_KV_W], avg, kg_ref[...], QK_EPS)
    kr = _rope(kn, c, sa, sb).astype(BF16)
    vt = att[:, ATTN_Q_W + ATTN_KV_W:].T.astype(BF16)
    ones = jnp.ones((VT_ROWS - ATTN_DH, vt.shape[1]), BF16)
    for g in range(ATTN_KV_HEADS):
        k_ref[0, g] = kr[:, ATTN_DH * g:ATTN_DH * (g + 1)]
        v_ref[0, g, 0:ATTN_DH, :] = vt[ATTN_DH * g:ATTN_DH * (g + 1), :]
        v_ref[0, g, ATTN_DH:VT_ROWS, :] = ones


def _kernel_a(x_ref, wg_ref, wu_ref, wd_ref, lng_ref, lnb_ref, win_ref, avg_ref, qg_ref, kg_ref,
              c_ref, sa_ref, sb_ref,
              h_ref, gqk_ref, gv_ref, gr_ref, z_ref, q_ref, k_ref, v_ref, att_ref):
    i = pl.program_id(1)
    n_tiles = pl.num_programs(1) - 1
    operands = functools.partial(_attn_operands, att_ref, avg_ref, qg_ref, kg_ref, c_ref, sa_ref, sb_ref,
                                 q_ref, k_ref, v_ref)

    @pl.when(i == 0)
    def _():
        att_ref[...] = jnp.zeros_like(att_ref)

    @pl.when(i < n_tiles)
    def _():
        operands()
        x = x_ref[0]
        f = _swiglu(x.astype(BF16), wg_ref, wu_ref, wd_ref)
        h = _layer_norm(DEEPNORM_ALPHA * x + 0.5 * f, lng_ref[...], lnb_ref[...])
        h_ref[0] = h
        proj = _dot(h.astype(BF16), win_ref[...])
        gqk_ref[0] = proj[:, OFF_GQK:OFF_GV]
        gv_ref[0] = proj[:, OFF_GV:OFF_GR].astype(BF16)
        gr_ref[0] = proj[:, OFF_GR:OFF_AQ]
        z_ref[0] = proj[:, OFF_Z:D_IN]
        att_ref[...] = proj[:, OFF_AQ:OFF_Z]

    pl.when(i == n_tiles)(operands)


def _call_a(x, w, tabs):
    bsz, n, _ = x.shape
    tm = TOKEN_TILE
    nt = n // tm
    cur = lambda i: jnp.minimum(i, nt - 1)
    prev = lambda i: jnp.maximum(i - 1, 0)
    tok = lambda width: pl.BlockSpec((1, tm, width), lambda b, i: (b, cur(i), 0))
    tab = pl.BlockSpec((tm, LANES), lambda b, i: (prev(i), 0))
    kv_spec = pl.BlockSpec((1, ATTN_KV_HEADS, tm, ATTN_DH), lambda b, i: (b, 0, prev(i), 0))
    vt_spec = pl.BlockSpec((1, ATTN_KV_HEADS, VT_ROWS, tm), lambda b, i: (b, 0, 0, prev(i)))
    qt_spec = pl.BlockSpec((1, tm // ATTN_UNIT_ROWS, ATTN_Q_W, ATTN_UNIT_ROWS), lambda b, i: (b, prev(i), 0, 0))
    out_shape = (
        jax.ShapeDtypeStruct((bsz, n, D_MODEL), F32),
        jax.ShapeDtypeStruct((bsz, n, 2 * GLA_QK_W), F32),
        jax.ShapeDtypeStruct((bsz, n, GLA_V_W), BF16),
        jax.ShapeDtypeStruct((bsz, n, GLA_V_W), F32),
        jax.ShapeDtypeStruct((bsz, n, Z_W), F32),
        jax.ShapeDtypeStruct((bsz, n // ATTN_UNIT_ROWS, ATTN_Q_W, ATTN_UNIT_ROWS), BF16),
        jax.ShapeDtypeStruct((bsz, ATTN_KV_HEADS, n, ATTN_DH), BF16),
        jax.ShapeDtypeStruct((bsz, ATTN_KV_HEADS, VT_ROWS, n), BF16),
    )
    return pl.pallas_call(
        _kernel_a,
        grid=(bsz, nt + 1),
        in_specs=[tok(D_MODEL),
                  _const_spec((D_MODEL, D_FF)), _const_spec((D_MODEL, D_FF)), _const_spec((D_FF, D_MODEL)),
                  _const_spec((1, D_MODEL)), _const_spec((1, D_MODEL)),
                  _const_spec((D_MODEL, D_IN)), _const_spec((AVG_W, AVG_W)),
                  _const_spec((1, ATTN_Q_W)), _const_spec((1, ATTN_KV_W)),
                  tab, tab, tab],
        out_specs=(tok(D_MODEL), tok(2 * GLA_QK_W), tok(GLA_V_W), tok(GLA_V_W), tok(Z_W), qt_spec,
                   kv_spec, vt_spec),
        out_shape=out_shape,
        scratch_shapes=[pltpu.VMEM((tm, OFF_Z - OFF_AQ), F32)],
        compiler_params=pltpu.CompilerParams(dimension_semantics=("arbitrary", "arbitrary"),
                                             vmem_limit_bytes=VMEM_LIMIT),
        name="ffn1_inproj",
    )(x, w["ffn1_wg"], w["ffn1_wu"], w["ffn1_wd"], w["ln1_g"], w["ln1_b"], w["w_in"], w["avg"],
      w["q_gain"], w["k_gain"], *tabs)


def _gla_log_decay_cumsum(z_ref, w2_ref, b2_ref, tri_ref):
    tn = z_ref.shape[1]
    pre = _dot(z_ref[0].astype(BF16), w2_ref[...]) + b2_ref[...]
    log_a = (jnp.minimum(pre, 0.0) - jnp.log(1.0 + jnp.exp(-jnp.abs(pre)))) * (1.0 / GLA_TAU)
    hi, lo = _split_bf16(log_a)
    tri = tri_ref[...]
    cum = jnp.concatenate(
        [_dot(tri, hi[r:r + CUM_TILE]) + _dot(tri, lo[r:r + CUM_TILE]) for r in range(0, tn, CUM_TILE)],
        axis=0)
    return cum * LOG2E


def _gla_chunk(gqk_ref, gv_ref, out_ref, out_row0, accumulate, cum, state, c, half_mask, keep, reverse):
    c_len = GLA_CHUNK
    r0 = c_len * c
    cc = cum[r0:r0 + c_len]
    mid = cc[c_len // 2:c_len // 2 + 1] if reverse else cc[c_len // 2 - 1:c_len // 2]
    last = cc[0:1] if reverse else cc[c_len - 1:c_len]
    qc = gqk_ref[0, r0:r0 + c_len, 0:GLA_QK_W] * GLA_SCALE
    kc = gqk_ref[0, r0:r0 + c_len, GLA_QK_W:2 * GLA_QK_W]
    vc = gv_ref[0, r0:r0 + c_len, :]
    rel = cc - mid
    q_in = qc * jnp.exp2(rel)
    k_in = (kc * jnp.exp2(-rel)).astype(BF16)
    k_up = kc * jnp.exp2(last - cc)
    q_st = qc * jnp.exp2(cc)
    state_b = state.astype(BF16)
    scores, o_state = [], []
    for col in range(GLA_QK_W // LANES):
        lanes = slice(LANES * col, LANES * (col + 1))
        stack = lambda x: jnp.concatenate([x[:, lanes].astype(BF16) * m for m in half_mask], axis=0)
        scores.append(_dot_nt(stack(q_in), k_in[:, lanes]))
        o_state.append(_dot(stack(q_st), state_b[lanes, :]))
    p = jnp.where(keep, jnp.concatenate(scores, axis=0), 0.0).astype(BF16)
    o_state = jnp.concatenate(o_state, axis=0)
    xt = jnp.concatenate([k_up, jnp.broadcast_to(last, (c_len, GLA_QK_W))], axis=0).T
    k_up_t = xt[:, 0:c_len].astype(BF16)
    decay_col = jnp.exp2(xt[:, c_len:c_len + 1])
    upd = []
    for h in range(GLA_HEADS):
        rows = slice(c_len * h, c_len * (h + 1))
        vh = vc[:, GLA_DV * h:GLA_DV * (h + 1)]
        o_h = _dot(p[rows], vh) + o_state[rows]
        dst = (0, pl.ds(pl.multiple_of(out_row0 + r0, c_len), c_len), slice(GLA_DV * h, GLA_DV * (h + 1)))
        out_ref[dst] = out_ref[dst] + o_h if accumulate else o_h
        upd.append(_dot(k_up_t[rows], vh))
    return decay_col * state + jnp.concatenate(upd, axis=0)


def _kernel_gla(gqkf_ref, gvf_ref, zf_ref, gqkb_ref, gvb_ref, zb_ref,
                w2f_ref, b2f_ref, w2b_ref, b2b_ref, lower_ref, upper_ref,
                o_ref, state_ref):
    j = pl.program_id(1)
    nb = pl.num_programs(1)

    @pl.when(j == 0)
    def _():
        state_ref[...] = jnp.zeros_like(state_ref)

    tn = gqkf_ref.shape[1]
    pl.when(j < nb // 2)(functools.partial(
        _gla_step, gqkf_ref, gvf_ref, zf_ref, gqkb_ref, gvb_ref, zb_ref, w2f_ref, b2f_ref, w2b_ref, b2b_ref,
        lower_ref, upper_ref, o_ref, state_ref, j * tn, (nb - 1 - j) * tn, False))
    pl.when(j >= nb // 2)(functools.partial(
        _gla_step, gqkf_ref, gvf_ref, zf_ref, gqkb_ref, gvb_ref, zb_ref, w2f_ref, b2f_ref, w2b_ref, b2b_ref,
        lower_ref, upper_ref, o_ref, state_ref, j * tn, (nb - 1 - j) * tn, True))


def _gla_step(gqkf_ref, gvf_ref, zf_ref, gqkb_ref, gvb_ref, zb_ref, w2f_ref, b2f_ref, w2b_ref, b2b_ref,
              lower_ref, upper_ref, o_ref, state_ref, row_f, row_b, accumulate):
    c_len = GLA_CHUNK
    n_chunks = gqkf_ref.shape[1] // c_len
    cum_f = _gla_log_decay_cumsum(zf_ref, w2f_ref, b2f_ref, lower_ref)
    cum_b = _gla_log_decay_cumsum(zb_ref, w2b_ref, b2b_ref, upper_ref)
    first_half = lax.broadcasted_iota(jnp.int32, (c_len, LANES), 1) < GLA_DK
    half_mask = [first_half.astype(BF16), (~first_half).astype(BF16)]
    row = lax.broadcasted_iota(jnp.int32, (GLA_HEADS * c_len, c_len), 0) % c_len
    col = lax.broadcasted_iota(jnp.int32, (GLA_HEADS * c_len, c_len), 1)
    state_f, state_b = state_ref[0], state_ref[1]
    for c in range(n_chunks):
        state_f = _gla_chunk(gqkf_ref, gvf_ref, o_ref, row_f, accumulate, cum_f, state_f, c, half_mask,
                             col <= row, False)
        state_b = _gla_chunk(gqkb_ref, gvb_ref, o_ref, row_b, accumulate, cum_b, state_b, n_chunks - 1 - c,
                             half_mask, col > row, True)
    state_ref[0] = state_f
    state_ref[1] = state_b


def _call_gla(gqk, gv, z, w):
    bsz, n, _ = gqk.shape
    tn = GLA_TILE
    nb = n // tn
    fwd = lambda width: pl.BlockSpec((1, tn, width), lambda b, j: (b, j, 0))
    bwd = lambda width: pl.BlockSpec((1, tn, width), lambda b, j: (b, nb - 1 - j, 0))
    assert nb % 2 == 0
    return pl.pallas_call(
        _kernel_gla,
        grid=(bsz, nb),
        in_specs=[fwd(2 * GLA_QK_W), fwd(GLA_V_W), fwd(Z_W), bwd(2 * GLA_QK_W), bwd(GLA_V_W), bwd(Z_W),
                  _const_spec((Z_W, GLA_QK_W)), _const_spec((1, GLA_QK_W)),
                  _const_spec((Z_W, GLA_QK_W)), _const_spec((1, GLA_QK_W)),
                  _const_spec((CUM_TILE, CUM_TILE)), _const_spec((CUM_TILE, CUM_TILE))],
        out_specs=pl.BlockSpec((1, n, GLA_V_W), lambda b, j: (b, 0, 0)),
        out_shape=jax.ShapeDtypeStruct((bsz, n, GLA_V_W), F32),
        scratch_shapes=[pltpu.VMEM((2, GLA_HEADS * GLA_DK, GLA_DV), F32)],
        compiler_params=pltpu.CompilerParams(dimension_semantics=("arbitrary", "arbitrary"),
                                             vmem_limit_bytes=VMEM_LIMIT),
        name="gla_bidir",
    )(gqk, gv, z, gqk, gv, z, w["w2f"], w["b2f"], w["w2b"], w["b2b"], w["tri_lower"], w["tri_upper"])


def _kernel_attn(flag_ref, qt_ref, k_ref, vt_ref, o_ref, st_ref, m_ref):
    n = k_ref.shape[2]
    sub = ATTN_UNIT_ROWS
    hpu = ATTN_UNIT_HEADS
    width = hpu * sub
    groups = ATTN_HEADS // ATTN_KV_HEADS // hpu
    n_units = (ATTN_Q_TILE // sub) * groups
    tiles = [slice(t, t + ATTN_KEY_TILE) for t in range(0, n, ATTN_KEY_TILE)]
    step = pl.program_id(2)
    last_step = pl.num_programs(2) - 1

    def unit_window(at_step, u):
        s, hg = divmod(u, groups)
        return at_step * (ATTN_Q_TILE // sub) + s, hg

    def run_phase(u_scores, scores_step, u_pv):
        if u_scores is not None:
            tile_idx, hg = unit_window(scores_step, u_scores)
            heads = [qt_ref[0, tile_idx, ATTN_DH * h:ATTN_DH * (h + 1), :]
                     for h in range(hpu * hg, hpu * (hg + 1))]
            q_cur = jnp.concatenate(heads, axis=1)
            m_run = None
        if u_pv is not None:
            m_fin = m_ref[u_pv]
            acc = jnp.zeros((VT_ROWS, width), F32)
        for idx in range(len(tiles)):
            if u_pv is not None:
                tile = tiles[idx]
                pt = jnp.exp2(st_ref[u_pv % 2, tile, :] - m_fin).astype(BF16)
                acc = acc + _dot(vt_ref[0, 0, :, tile], pt)
            if u_scores is not None:
                tile = tiles[(idx + len(tiles) // 2) % len(tiles)]
                st = _dot(k_ref[0, 0, tile, :], q_cur)
                st_ref[u_scores % 2, tile, :] = st
                m_tile = jnp.max(st, axis=0, keepdims=True)
                m_run = m_tile if m_run is None else jnp.maximum(m_run, m_tile)
        if u_scores is not None:
            m_ref[u_scores] = m_run
        if u_pv is not None:
            tile_idx, hg = unit_window(step, u_pv)
            row0 = pl.multiple_of(tile_idx * sub, sub)
            o2 = (acc[0:ATTN_DH] / acc[ATTN_DH:ATTN_DH + 1]).T
            o_ref[0, pl.ds(row0, sub), ATTN_DH * hpu * hg:ATTN_DH * hpu * (hg + 1)] = (
                jnp.concatenate([o2[sub * i:sub * (i + 1)] for i in range(hpu)], axis=1).astype(BF16))

    pl.when(step == 0)(functools.partial(run_phase, 0, step, None))
    for u in range(n_units - 1):
        pl.when(flag_ref[u] == 0)(functools.partial(run_phase, u + 1, step, u))
    pl.when(step < last_step)(functools.partial(run_phase, 0, step + 1, n_units - 1))
    pl.when(step == last_step)(functools.partial(run_phase, None, None, n_units - 1))


def _call_attn(qt, k, vt):
    bsz, n = qt.shape[0], k.shape[2]
    group_w = ATTN_Q_W // ATTN_KV_HEADS
    width = ATTN_UNIT_HEADS * ATTN_UNIT_ROWS
    n_units = (ATTN_Q_TILE // ATTN_UNIT_ROWS) * (group_w // (ATTN_UNIT_HEADS * ATTN_DH))
    assert n_units % 2 == 0
    qt_spec = pl.BlockSpec((1, n // ATTN_UNIT_ROWS, group_w, ATTN_UNIT_ROWS), lambda b, g, i: (b, 0, g, 0))
    o_spec = pl.BlockSpec((1, n, group_w), lambda b, g, i: (b, 0, g))
    k_spec = pl.BlockSpec((1, 1, n, ATTN_DH), lambda b, g, i: (b, g, 0, 0))
    vt_spec = pl.BlockSpec((1, 1, VT_ROWS, n), lambda b, g, i: (b, g, 0, 0))
    return pl.pallas_call(
        _kernel_attn,
        grid=(bsz, ATTN_KV_HEADS, n // ATTN_Q_TILE),
        in_specs=[pl.BlockSpec(memory_space=pltpu.SMEM), qt_spec, k_spec, vt_spec],
        out_specs=o_spec,
        out_shape=jax.ShapeDtypeStruct((bsz, n, ATTN_Q_W), BF16),
        scratch_shapes=[pltpu.VMEM((2, n, width), F32),
                        pltpu.VMEM((n_units, 1, width), F32)],
        compiler_params=pltpu.CompilerParams(
            dimension_semantics=("arbitrary", "arbitrary", "arbitrary"),
            vmem_limit_bytes=VMEM_LIMIT),
        name="gqa_attention",
    )(jnp.zeros((n_units,), jnp.int32), qt, k, vt)


def _kernel_b(h1_ref, og_ref, gr_ref, oa_ref, p_ref, gn_ref, wog_ref, woa_ref,
              ln2g_ref, ln2b_ref, wg_ref, wu_ref, wd_ref, ln3g_ref, ln3b_ref, wpg_ref, bpg_ref, wpe_ref,
              out_ref):
    o = og_ref[0]
    gn = gn_ref[...]
    normed = []
    for h in range(GLA_HEADS):
        oh = o[:, GLA_DV * h:GLA_DV * (h + 1)]
        ms = jnp.mean(oh * oh, axis=-1, keepdims=True)
        normed.append(oh * lax.rsqrt(ms + GN_EPS) * gn)
    gr = gr_ref[0]
    o_gla = jnp.concatenate(normed, axis=1) * (gr * jax.nn.sigmoid(gr))
    mix = _dot(o_gla.astype(BF16), wog_ref[...]) + _dot(oa_ref[0], woa_ref[...])
    h2 = _layer_norm(DEEPNORM_ALPHA * h1_ref[0] + mix, ln2g_ref[...], ln2b_ref[...])
    f = _swiglu(h2.astype(BF16), wg_ref, wu_ref, wd_ref)
    h3 = _layer_norm(DEEPNORM_ALPHA * h2 + 0.5 * f, ln3g_ref[...], ln3b_ref[...])
    gate = jax.nn.sigmoid(_dot(h3.astype(BF16), wpg_ref[...]) + bpg_ref[...])
    out_ref[0] = h3 + gate * _dot(p_ref[0].astype(BF16), wpe_ref[...])


def _call_b(h1, o_gla, gr, o_att, p, w):
    bsz, n, _ = h1.shape
    tm = TOKEN_TILE
    tok = lambda width: pl.BlockSpec((1, tm, width), lambda b, i: (b, i, 0))
    return pl.pallas_call(
        _kernel_b,
        grid=(bsz, n // tm),
        in_specs=[tok(D_MODEL), tok(GLA_V_W), tok(GLA_V_W), tok(ATTN_Q_W), tok(P_DIM),
                  _const_spec((1, GLA_DV)),
                  _const_spec((GLA_V_W, D_MODEL)), _const_spec((ATTN_Q_W, D_MODEL)),
                  _const_spec((1, D_MODEL)), _const_spec((1, D_MODEL)),
                  _const_spec((D_MODEL, D_FF)), _const_spec((D_MODEL, D_FF)), _const_spec((D_FF, D_MODEL)),
                  _const_spec((1, D_MODEL)), _const_spec((1, D_MODEL)),
                  _const_spec((D_MODEL, D_MODEL)), _const_spec((1, D_MODEL)), _const_spec((P_DIM, D_MODEL))],
        out_specs=tok(D_MODEL),
        out_shape=jax.ShapeDtypeStruct((bsz, n, D_MODEL), F32),
        compiler_params=pltpu.CompilerParams(dimension_semantics=("arbitrary", "arbitrary"),
                                             vmem_limit_bytes=VMEM_LIMIT),
        name="outproj_ffn2_embed",
    )(h1, o_gla, gr, o_att, p, w["gn_g"], w["w_out_gla"], w["w_out_att"], w["ln2_g"], w["ln2_b"],
      w["ffn2_wg"], w["ffn2_wu"], w["ffn2_wd"], w["ln3_g"], w["ln3_b"], w["w_pg"], w["b_pg"], w["w_pe"])


def _rope_tables(n):
    t = jnp.arange(n, dtype=jnp.int32)
    row = (t // GRID_W).astype(F32)
    col = (t % GRID_W).astype(F32)
    axis_dim = ATTN_DH // 2
    inv_freq = ROPE_THETA ** (-jnp.arange(0, axis_dim, 2, dtype=F32) / axis_dim)
    lane = jnp.arange(LANES, dtype=jnp.int32) % ATTN_DH
    freq = inv_freq[lane % ROPE_HALF]
    pos = jnp.where((lane // axis_dim)[None, :] == 0, row[:, None], col[:, None])
    ang = pos * freq[None, :]
    first_half = ((lane % axis_dim) // ROPE_HALF == 0)[None, :]
    cos, sin = jnp.cos(ang), jnp.sin(ang)
    return cos, jnp.where(first_half, -sin, 0.0), jnp.where(first_half, 0.0, sin)


def _prepare(ffn1_wg, ffn1_wu, ffn1_wd, ln1_g, ln1_b, w_in, gla_w2f, gla_b2f, gla_w2b, gla_b2b,
             gla_gn_g, q_norm_g, k_norm_g, w_out, ln2_g, ln2_b, ffn2_wg, ffn2_wu, ffn2_wd,
             ln3_g, ln3_b, w_pg, b_pg, w_pe):
    row = lambda v: v.reshape(1, -1).astype(F32)
    z0 = 2 * GLA_QK_W + 2 * GLA_V_W
    w_in_r = jnp.concatenate([w_in[:, :z0], w_in[:, z0 + Z_W:], w_in[:, z0:z0 + Z_W]], axis=1)
    zeros = jnp.zeros((GLA_GATE_RANK, GLA_QK_W), F32)
    idx = jnp.arange(AVG_W)
    avg =jnp.where((idx[:, None] // ATTN_DH) == (idx[None, :] // ATTN_DH), 1.0 / ATTN_DH, 0.0)
    ci = jnp.arange(CUM_TILE)
    same_chunk = (ci[:, None] // GLA_CHUNK) == (ci[None, :] // GLA_CHUNK)
    lower = jnp.where(same_chunk & (ci[None, :] <= ci[:, None]), 1.0, 0.0)
    return {
        "ffn1_wg": ffn1_wg.astype(BF16), "ffn1_wu": ffn1_wu.astype(BF16), "ffn1_wd": ffn1_wd.astype(BF16),
        "ln1_g": row(ln1_g), "ln1_b": row(ln1_b),
        "w_in": w_in_r.astype(BF16), "avg": avg.astype(BF16),
        "q_gain": row(jnp.tile(q_norm_g, ATTN_HEADS)), "k_gain": row(jnp.tile(k_norm_g, ATTN_KV_HEADS)),
        "w2f": jnp.concatenate([gla_w2f, zeros], axis=0).astype(BF16), "b2f": row(gla_b2f),
        "w2b": jnp.concatenate([zeros, gla_w2b], axis=0).astype(BF16), "b2b": row(gla_b2b),
        "tri_lower": lower.astype(BF16), "tri_upper": lower.T.astype(BF16),
        "gn_g": row(gla_gn_g),
        "w_out_gla": w_out[:GLA_V_W].astype(BF16), "w_out_att": w_out[GLA_V_W:].astype(BF16),
        "ln2_g": row(ln2_g), "ln2_b": row(ln2_b),
        "ffn2_wg": ffn2_wg.astype(BF16), "ffn2_wu": ffn2_wu.astype(BF16), "ffn2_wd": ffn2_wd.astype(BF16),
        "ln3_g": row(ln3_g), "ln3_b": row(ln3_b),
        "w_pg": w_pg.astype(BF16), "b_pg": row(b_pg), "w_pe": w_pe.astype(BF16),
    }


def _encoder_layer(x, p, w):
    n = x.shape[1]
    assert n % GLA_TILE == 0 and n % ATTN_Q_TILE == 0 and n % TOKEN_TILE == 0 and n % GRID_W == 0
    h1, gqk, gv, gr, z, q, k, v = _call_a(x, w, _rope_tables(n))
    o_gla = _call_gla(gqk, gv, z, w)
    o_att = _call_attn(q, k, v)
    return _call_b(h1, o_gla, gr, o_att, p, w)


def kernel(x_prompt, x_sample, p_prompt, p_sample, ffn1_wg, ffn1_wu, ffn1_wd, ln1_g, ln1_b, w_in,
           gla_w2f, gla_b2f, gla_w2b, gla_b2b, gla_gn_g, q_norm_g, k_norm_g, w_out, ln2_g, ln2_b,
           ffn2_wg, ffn2_wu, ffn2_wd, ln3_g, ln3_b, w_pg, b_pg, w_pe):
    y_prompt, y_sample = x_prompt, x_sample
    for i in range(DEPTH):
        w = _prepare(ffn1_wg[i], ffn1_wu[i], ffn1_wd[i], ln1_g[i], ln1_b[i], w_in[i],
                     gla_w2f[i], gla_b2f[i], gla_w2b[i], gla_b2b[i], gla_gn_g[i], q_norm_g[i],
                     k_norm_g[i], w_out[i], ln2_g[i], ln2_b[i], ffn2_wg[i], ffn2_wu[i], ffn2_wd[i],
                     ln3_g[i], ln3_b[i], w_pg[i], b_pg[i], w_pe[i])
        y_prompt = _encoder_layer(y_prompt, p_prompt[i], w)
        y_sample = _encoder_layer(y_sample, p_sample[i], w)
    return (y_prompt, y_sample)
```

```python
import functools

import jax
import jax.numpy as jnp
from jax import lax
from jax.experimental import pallas as pl
from jax.experimental.pallas import tpu as pltpu

F32 = jnp.float32
BF16 = jnp.bfloat16

D_MODEL = 1024
D_FF = 2816
P_DIM = 256
GRID_W = 64
DEPTH = 1
GLA_HEADS = 4
GLA_DK = 64
GLA_DV = 128
GLA_GATE_RANK = 16
GLA_TAU = 16.0
GLA_CHUNK = 64
ATTN_HEADS = 8
ATTN_KV_HEADS = 2
ATTN_DH = 64
ROPE_THETA = 10000.0
LN_EPS = 1e-5
QK_EPS = 1e-6
GN_EPS = 1e-5
DEEPNORM_ALPHA = (2.0 * DEPTH) ** 0.25

GLA_QK_W = GLA_HEADS * GLA_DK
GLA_V_W = GLA_HEADS * GLA_DV
ATTN_Q_W = ATTN_HEADS * ATTN_DH
ATTN_KV_W = ATTN_KV_HEADS * ATTN_DH
Z_W = 2 * GLA_GATE_RANK
OFF_GQK = 0
OFF_GV = OFF_GQK + 2 * GLA_QK_W
OFF_GR = OFF_GV + GLA_V_W
OFF_AQ = OFF_GR + GLA_V_W
OFF_AK = OFF_AQ + ATTN_Q_W
OFF_AV = OFF_AK + ATTN_KV_W
OFF_Z = OFF_AV + ATTN_KV_W
D_IN = OFF_Z + Z_W

LANES = 128
AVG_W = 256
ROPE_HALF = ATTN_DH // 4
LOG2E = 1.4426950408889634
QK_SCALE = ATTN_DH ** -0.5 * LOG2E
GLA_SCALE = GLA_DK ** -0.5

TOKEN_TILE = 512
FF_CHUNKS = ((0, 1536), (1536, D_FF))
GLA_TILE = 1024
CUM_TILE = 256
ATTN_Q_TILE = 2048
ATTN_UNIT_ROWS = 256
ATTN_UNIT_HEADS = 4
ATTN_KEY_TILE = 256
VT_ROWS = ATTN_DH + 16
VMEM_LIMIT = 56 * 1024 * 1024


def _const_spec(shape):
    return pl.BlockSpec(shape, lambda *_: (0,) * len(shape), pipeline_mode=pl.Buffered(1))


def _dot(a, b):
    return jnp.dot(a, b, preferred_element_type=F32)


def _dot_nt(a, b):
    return lax.dot_general(a, b, (((1,), (1,)), ((), ())), preferred_element_type=F32)


def _split_bf16(x):
    hi = x.astype(BF16)
    lo = (x - hi.astype(F32)).astype(BF16)
    return hi, lo


def _layer_norm(y, g, b):
    mu = jnp.mean(y, axis=-1, keepdims=True)
    d = y - mu
    var = jnp.mean(d * d, axis=-1, keepdims=True)
    return d * lax.rsqrt(var + LN_EPS) * g + b


def _swiglu(xb, wg_ref, wu_ref, wd_ref):
    out = None
    for lo, hi in FF_CHUNKS:
        g = _dot(xb, wg_ref[:, lo:hi])
        u = _dot(xb, wu_ref[:, lo:hi])
        hid = (g * jax.nn.sigmoid(g)) * u
        part = _dot(hid.astype(BF16), wd_ref[lo:hi, :])
        out = part if out is None else out + part
    return out


def _head_rms(x, avg, gain, eps):
    hi, lo = _split_bf16(x * x)
    w = min(x.shape[1], AVG_W)
    a = avg[:w, :w]
    ms = jnp.concatenate([_dot(hi[:, c:c + w], a) + _dot(lo[:, c:c + w], a) for c in range(0, x.shape[1], w)],
                         axis=1)
    return x * lax.rsqrt(ms + eps) * gain


def _rope(xs, c, sa, sb):
    return (xs * c + pltpu.roll(xs, LANES - ROPE_HALF, 1) * sa
            + pltpu.roll(xs, ROPE_HALF, 1) * sb)


def _attn_operands(att_ref, avg_ref, qg_ref, kg_ref, c_ref, sa_ref, sb_ref, q_ref, k_ref, v_ref):
    att = att_ref[...]
    c, sa, sb = c_ref[...], sa_ref[...], sb_ref[...]
    avg = avg_ref[...]
    qn = _head_rms(att[:, 0:ATTN_Q_W], avg, qg_ref[...], QK_EPS)
    for j in range(ATTN_Q_W // LANES):
        sl = slice(LANES * j, LANES * (j + 1))
        qr = _rope(qn[:, sl], c, sa, sb) * QK_SCALE
        for t in range(qr.shape[0] // ATTN_UNIT_ROWS):
            q_ref[0, t, sl, :] = qr[ATTN_UNIT_ROWS * t:ATTN_UNIT_ROWS * (t + 1), :].T.astype(BF16)
    kn = _head_rms(att[:, ATTN_Q_W:ATTN_Q_W + ATTN_KV_W], avg, kg_ref[...], QK_EPS)
    kr = _rope(kn, c, sa, sb).astype(BF16)
    vt = att[:, ATTN_Q_W + ATTN_KV_W:].T.astype(BF16)
    ones = jnp.ones((VT_ROWS - ATTN_DH, vt.shape[1]), BF16)
    for g in range(ATTN_KV_HEADS):
        k_ref[0, g] = kr[:, ATTN_DH * g:ATTN_DH * (g + 1)]
        v_ref[0, g, 0:ATTN_DH, :] = vt[ATTN_DH * g:ATTN_DH * (g + 1), :]
        v_ref[0, g, ATTN_DH:VT_ROWS, :] = ones


def _kernel_a(x_ref, wg_ref, wu_ref, wd_ref, lng_ref, lnb_ref, win_ref, avg_ref, qg_ref, kg_ref,
              c_ref, sa_ref, sb_ref,
              h_ref, gqk_ref, gv_ref, gr_ref, z_ref, q_ref, k_ref, v_ref, att_ref):
    i = pl.program_id(1)
    n_tiles = pl.num_programs(1) - 1
    operands = functools.partial(_attn_operands, att_ref, avg_ref, qg_ref, kg_ref, c_ref, sa_ref, sb_ref,
                                 q_ref, k_ref, v_ref)

    @pl.when(i == 0)
    def _():
        att_ref[...] = jnp.zeros_like(att_ref)

    @pl.when(i < n_tiles)
    def _():
        operands()
        x = x_ref[0]
        f = _swiglu(x.astype(BF16), wg_ref, wu_ref, wd_ref)
        h = _layer_norm(DEEPNORM_ALPHA * x + 0.5 * f, lng_ref[...], lnb_ref[...])
        h_ref[0] = h
        proj = _dot(h.astype(BF16), win_ref[...])
        gqk_ref[0] = proj[:, OFF_GQK:OFF_GV]
        gv_ref[0] = proj[:, OFF_GV:OFF_GR].astype(BF16)
        gr_ref[0] = proj[:, OFF_GR:OFF_AQ]
        z_ref[0] = proj[:, OFF_Z:D_IN]
        att_ref[...] = proj[:, OFF_AQ:OFF_Z]

    pl.when(i == n_tiles)(operands)


def _call_a(x, w, tabs):
    bsz, n, _ = x.shape
    tm = TOKEN_TILE
    nt = n // tm
    cur = lambda i: jnp.minimum(i, nt - 1)
    prev = lambda i: jnp.maximum(i - 1, 0)
    tok = lambda width: pl.BlockSpec((1, tm, width), lambda b, i: (b, cur(i), 0))
    tab = pl.BlockSpec((tm, LANES), lambda b, i: (prev(i), 0))
    kv_spec = pl.BlockSpec((1, ATTN_KV_HEADS, tm, ATTN_DH), lambda b, i: (b, 0, prev(i), 0))
    vt_spec = pl.BlockSpec((1, ATTN_KV_HEADS, VT_ROWS, tm), lambda b, i: (b, 0, 0, prev(i)))
    qt_spec = pl.BlockSpec((1, tm // ATTN_UNIT_ROWS, ATTN_Q_W, ATTN_UNIT_ROWS), lambda b, i: (b, prev(i), 0, 0))
    out_shape = (
        jax.ShapeDtypeStruct((bsz, n, D_MODEL), F32),
        jax.ShapeDtypeStruct((bsz, n, 2 * GLA_QK_W), F32),
        jax.ShapeDtypeStruct((bsz, n, GLA_V_W), BF16),
        jax.ShapeDtypeStruct((bsz, n, GLA_V_W), F32),
        jax.ShapeDtypeStruct((bsz, n, Z_W), F32),
        jax.ShapeDtypeStruct((bsz, n // ATTN_UNIT_ROWS, ATTN_Q_W, ATTN_UNIT_ROWS), BF16),
        jax.ShapeDtypeStruct((bsz, ATTN_KV_HEADS, n, ATTN_DH), BF16),
        jax.ShapeDtypeStruct((bsz, ATTN_KV_HEADS, VT_ROWS, n), BF16),
    )
    return pl.pallas_call(
        _kernel_a,
        grid=(bsz, nt + 1),
        in_specs=[tok(D_MODEL),
                  _const_spec((D_MODEL, D_FF)), _const_spec((D_MODEL, D_FF)), _const_spec((D_FF, D_MODEL)),
                  _const_spec((1, D_MODEL)), _const_spec((1, D_MODEL)),
                  _const_spec((D_MODEL, D_IN)), _const_spec((AVG_W, AVG_W)),
                  _const_spec((1, ATTN_Q_W)), _const_spec((1, ATTN_KV_W)),
                  tab, tab, tab],
        out_specs=(tok(D_MODEL), tok(2 * GLA_QK_W), tok(GLA_V_W), tok(GLA_V_W), tok(Z_W), qt_spec,
                   kv_spec, vt_spec),
        out_shape=out_shape,
        scratch_shapes=[pltpu.VMEM((tm, OFF_Z - OFF_AQ), F32)],
        compiler_params=pltpu.CompilerParams(dimension_semantics=("arbitrary", "arbitrary"),
                                             vmem_limit_bytes=VMEM_LIMIT),
        name="ffn1_inproj",
    )(x, w["ffn1_wg"], w["ffn1_wu"], w["ffn1_wd"], w["ln1_g"], w["ln1_b"], w["w_in"], w["avg"],
      w["q_gain"], w["k_gain"], *tabs)


def _gla_log_decay_cumsum(z_ref, w2_ref, b2_ref, tri_ref):
    tn = z_ref.shape[1]
    pre = _dot(z_ref[0].astype(BF16), w2_ref[...]) + b2_ref[...]
    log_a = (jnp.minimum(pre, 0.0) - jnp.log(1.0 + jnp.exp(-jnp.abs(pre)))) * (1.0 / GLA_TAU)
    hi, lo = _split_bf16(log_a)
    tri = tri_ref[...]
    cum = jnp.concatenate(
        [_dot(tri, hi[r:r + CUM_TILE]) + _dot(tri, lo[r:r + CUM_TILE]) for r in range(0, tn, CUM_TILE)],
        axis=0)
    return cum * LOG2E


def _gla_chunk(gqk_ref, gv_ref, out_ref, out_row0, accumulate, cum, state, c, half_mask, keep, reverse):
    c_len = GLA_CHUNK
    r0 = c_len * c
    cc = cum[r0:r0 + c_len]
    mid = cc[c_len // 2:c_len // 2 + 1] if reverse else cc[c_len // 2 - 1:c_len // 2]
    last = cc[0:1] if reverse else cc[c_len - 1:c_len]
    qc = gqk_ref[0, r0:r0 + c_len, 0:GLA_QK_W] * GLA_SCALE
    kc = gqk_ref[0, r0:r0 + c_len, GLA_QK_W:2 * GLA_QK_W]
    vc = gv_ref[0, r0:r0 + c_len, :]
    rel = cc - mid
    q_in = qc * jnp.exp2(rel)
    k_in = (kc * jnp.exp2(-rel)).astype(BF16)
    k_up = kc * jnp.exp2(last - cc)
    q_st = qc * jnp.exp2(cc)
    state_b = state.astype(BF16)
    scores, o_state = [], []
    for col in range(GLA_QK_W // LANES):
        lanes = slice(LANES * col, LANES * (col + 1))
        stack = lambda x: jnp.concatenate([x[:, lanes].astype(BF16) * m for m in half_mask], axis=0)
        scores.append(_dot_nt(stack(q_in), k_in[:, lanes]))
        o_state.append(_dot(stack(q_st), state_b[lanes, :]))
    p = jnp.where(keep, jnp.concatenate(scores, axis=0), 0.0).astype(BF16)
    o_state = jnp.concatenate(o_state, axis=0)
    xt = jnp.concatenate([k_up, jnp.broadcast_to(last, (c_len, GLA_QK_W))], axis=0).T
    k_up_t = xt[:, 0:c_len].astype(BF16)
    decay_col = jnp.exp2(xt[:, c_len:c_len + 1])
    upd = []
    for h in range(GLA_HEADS):
        rows = slice(c_len * h, c_len * (h + 1))
        vh = vc[:, GLA_DV * h:GLA_DV * (h + 1)]
        o_h = _dot(p[rows], vh) + o_state[rows]
        dst = (0, pl.ds(pl.multiple_of(out_row0 + r0, c_len), c_len), slice(GLA_DV * h, GLA_DV * (h + 1)))
        out_ref[dst] = out_ref[dst] + o_h if accumulate else o_h
        upd.append(_dot(k_up_t[rows], vh))
    return decay_col * state + jnp.concatenate(upd, axis=0)


def _kernel_gla(gqkf_ref, gvf_ref, zf_ref, gqkb_ref, gvb_ref, zb_ref,
                w2f_ref, b2f_ref, w2b_ref, b2b_ref, lower_ref, upper_ref,
                o_ref, state_ref):
    j = pl.program_id(1)
    nb = pl.num_programs(1)

    @pl.when(j == 0)
    def _():
        state_ref[...] = jnp.zeros_like(state_ref)

    tn = gqkf_ref.shape[1]
    pl.when(j < nb // 2)(functools.partial(
        _gla_step, gqkf_ref, gvf_ref, zf_ref, gqkb_ref, gvb_ref, zb_ref, w2f_ref, b2f_ref, w2b_ref, b2b_ref,
        lower_ref, upper_ref, o_ref, state_ref, j * tn, (nb - 1 - j) * tn, False))
    pl.when(j >= nb // 2)(functools.partial(
        _gla_step, gqkf_ref, gvf_ref, zf_ref, gqkb_ref, gvb_ref, zb_ref, w2f_ref, b2f_ref, w2b_ref, b2b_ref,
        lower_ref, upper_ref, o_ref, state_ref, j * tn, (nb - 1 - j) * tn, True))


def _gla_step(gqkf_ref, gvf_ref, zf_ref, gqkb_ref, gvb_ref, zb_ref, w2f_ref, b2f_ref, w2b_ref, b2b_ref,
              lower_ref, upper_ref, o_ref, state_ref, row_f, row_b, accumulate):
    c_len = GLA_CHUNK
    n_chunks = gqkf_ref.shape[1] // c_len
    cum_f = _gla_log_decay_cumsum(zf_ref, w2f_ref, b2f_ref, lower_ref)
    cum_b = _gla_log_decay_cumsum(zb_ref, w2b_ref, b2b_ref, upper_ref)
    first_half = lax.broadcasted_iota(jnp.int32, (c_len, LANES), 1) < GLA_DK
    half_mask = [first_half.astype(BF16), (~first_half).astype(BF16)]
    row = lax.broadcasted_iota(jnp.int32, (GLA_HEADS * c_len, c_len), 0) % c_len
    col = lax.broadcasted_iota(jnp.int32, (GLA_HEADS * c_len, c_len), 1)
    state_f, state_b = state_ref[0], state_ref[1]
    for c in range(n_chunks):
        state_f = _gla_chunk(gqkf_ref, gvf_ref, o_ref, row_f, accumulate, cum_f, state_f, c, half_mask,
                             col <= row, False)
        state_b = _gla_chunk(gqkb_ref, gvb_ref, o_ref, row_b, accumulate, cum_b, state_b, n_chunks - 1 - c,
                             half_mask, col > row, True)
    state_ref[0] = state_f
    state_ref[1] = state_b


def _call_gla(gqk, gv, z, w):
    bsz, n, _ = gqk.shape
    tn = GLA_TILE
    nb = n // tn
    fwd = lambda width: pl.BlockSpec((1, tn, width), lambda b, j: (b, j, 0))
    bwd = lambda width: pl.BlockSpec((1, tn, width), lambda b, j: (b, nb - 1 - j, 0))
    assert nb % 2 == 0
    return pl.pallas_call(
        _kernel_gla,
        grid=(bsz, nb),
        in_specs=[fwd(2 * GLA_QK_W), fwd(GLA_V_W), fwd(Z_W), bwd(2 * GLA_QK_W), bwd(GLA_V_W), bwd(Z_W),
                  _const_spec((Z_W, GLA_QK_W)), _const_spec((1, GLA_QK_W)),
                  _const_spec((Z_W, GLA_QK_W)), _const_spec((1, GLA_QK_W)),
                  _const_spec((CUM_TILE, CUM_TILE)), _const_spec((CUM_TILE, CUM_TILE))],
        out_specs=pl.BlockSpec((1, n, GLA_V_W), lambda b, j: (b, 0, 0)),
        out_shape=jax.ShapeDtypeStruct((bsz, n, GLA_V_W), F32),
        scratch_shapes=[pltpu.VMEM((2, GLA_HEADS * GLA_DK, GLA_DV), F32)],
        compiler_params=pltpu.CompilerParams(dimension_semantics=("arbitrary", "arbitrary"),
                                             vmem_limit_bytes=VMEM_LIMIT),
        name="gla_bidir",
    )(gqk, gv, z, gqk, gv, z, w["w2f"], w["b2f"], w["w2b"], w["b2b"], w["tri_lower"], w["tri_upper"])


def _kernel_attn(flag_ref, qt_ref, k_ref, vt_ref, o_ref, st_ref, m_ref):
    n = k_ref.shape[2]
    sub = ATTN_UNIT_ROWS
    hpu = ATTN_UNIT_HEADS
    width = hpu * sub
    groups = ATTN_HEADS // ATTN_KV_HEADS // hpu
    n_units = (ATTN_Q_TILE // sub) * groups
    tiles = [slice(t, t + ATTN_KEY_TILE) for t in range(0, n, ATTN_KEY_TILE)]
    step = pl.program_id(2)
    last_step = pl.num_programs(2) - 1

    def unit_window(at_step, u):
        s, hg = divmod(u, groups)
        return at_step * (ATTN_Q_TILE // sub) + s, hg

    def run_phase(u_scores, scores_step, u_pv):
        if u_scores is not None:
            tile_idx, hg = unit_window(scores_step, u_scores)
            heads = [qt_ref[0, tile_idx, ATTN_DH * h:ATTN_DH * (h + 1), :]
                     for h in range(hpu * hg, hpu * (hg + 1))]
            q_cur = jnp.concatenate(heads, axis=1)
            m_run = None
        if u_pv is not None:
            m_fin = m_ref[u_pv]
            acc = jnp.zeros((VT_ROWS, width), F32)
        for idx in range(len(tiles)):
            if u_pv is not None:
                tile = tiles[idx]
                pt = jnp.exp2(st_ref[u_pv % 2, tile, :] - m_fin).astype(BF16)
                acc = acc + _dot(vt_ref[0, 0, :, tile], pt)
            if u_scores is not None:
                tile = tiles[(idx + len(tiles) // 2) % len(tiles)]
                st = _dot(k_ref[0, 0, tile, :], q_cur)
                st_ref[u_scores % 2, tile, :] = st
                m_tile = jnp.max(st, axis=0, keepdims=True)
                m_run = m_tile if m_run is None else jnp.maximum(m_run, m_tile)
        if u_scores is not None:
            m_ref[u_scores] = m_run
        if u_pv is not None:
            tile_idx, hg = unit_window(step, u_pv)
            row0 = pl.multiple_of(tile_idx * sub, sub)
            o2 = (acc[0:ATTN_DH] / acc[ATTN_DH:ATTN_DH + 1]).T
            o_ref[0, pl.ds(row0, sub), ATTN_DH * hpu * hg:ATTN_DH * hpu * (hg + 1)] = (
                jnp.concatenate([o2[sub * i:sub * (i + 1)] for i in range(hpu)], axis=1).astype(BF16))

    pl.when(step == 0)(functools.partial(run_phase, 0, step, None))
    for u in range(n_units - 1):
        pl.when(flag_ref[u] == 0)(functools.partial(run_phase, u + 1, step, u))
    pl.when(step < last_step)(functools.partial(run_phase, 0, step + 1, n_units - 1))
    pl.when(step == last_step)(functools.partial(run_phase, None, None, n_units - 1))


def _call_attn(qt, k, vt):
    bsz, n = qt.shape[0], k.shape[2]
    group_w = ATTN_Q_W // ATTN_KV_HEADS
    width = ATTN_UNIT_HEADS * ATTN_UNIT_ROWS
    n_units = (ATTN_Q_TILE // ATTN_UNIT_ROWS) * (group_w // (ATTN_UNIT_HEADS * ATTN_DH))
    assert n_units % 2 == 0
    qt_spec = pl.BlockSpec((1, n // ATTN_UNIT_ROWS, group_w, ATTN_UNIT_ROWS), lambda b, g, i: (b, 0, g, 0))
    o_spec = pl.BlockSpec((1, n, group_w), lambda b, g, i: (b, 0, g))
    k_spec = pl.BlockSpec((1, 1, n, ATTN_DH), lambda b, g, i: (b, g, 0, 0))
    vt_spec = pl.BlockSpec((1, 1, VT_ROWS, n), lambda b, g, i: (b, g, 0, 0))
    return pl.pallas_call(
        _kernel_attn,
        grid=(bsz, ATTN_KV_HEADS, n // ATTN_Q_TILE),
        in_specs=[pl.BlockSpec(memory_space=pltpu.SMEM), qt_spec, k_spec, vt_spec],
        out_specs=o_spec,
        out_shape=jax.ShapeDtypeStruct((bsz, n, ATTN_Q_W), BF16),
        scratch_shapes=[pltpu.VMEM((2, n, width), F32),
                        pltpu.VMEM((n_units, 1, width), F32)],
        compiler_params=pltpu.CompilerParams(
            dimension_semantics=("arbitrary", "arbitrary", "arbitrary"),
            vmem_limit_bytes=VMEM_LIMIT),
        name="gqa_attention",
    )(jnp.zeros((n_units,), jnp.int32), qt, k, vt)


def _kernel_b(h1_ref, og_ref, gr_ref, oa_ref, p_ref, gn_ref, wog_ref, woa_ref,
              ln2g_ref, ln2b_ref, wg_ref, wu_ref, wd_ref, ln3g_ref, ln3b_ref, wpg_ref, bpg_ref, wpe_ref,
              out_ref):
    o = og_ref[0]
    gn = gn_ref[...]
    normed = []
    for h in range(GLA_HEADS):
        oh = o[:, GLA_DV * h:GLA_DV * (h + 1)]
        ms = jnp.mean(oh * oh, axis=-1, keepdims=True)
        normed.append(oh * lax.rsqrt(ms + GN_EPS) * gn)
    gr = gr_ref[0]
    o_gla = jnp.concatenate(normed, axis=1) * (gr * jax.nn.sigmoid(gr))
    mix = _dot(o_gla.astype(BF16), wog_ref[...]) + _dot(oa_ref[0], woa_ref[...])
    h2 = _layer_norm(DEEPNORM_ALPHA * h1_ref[0] + mix, ln2g_ref[...], ln2b_ref[...])
    f = _swiglu(h2.astype(BF16), wg_ref, wu_ref, wd_ref)
    h3 = _layer_norm(DEEPNORM_ALPHA * h2 + 0.5 * f, ln3g_ref[...], ln3b_ref[...])
    gate = jax.nn.sigmoid(_dot(h3.astype(BF16), wpg_ref[...]) + bpg_ref[...])
    out_ref[0] = h3 + gate * _dot(p_ref[0].astype(BF16), wpe_ref[...])


def _call_b(h1, o_gla, gr, o_att, p, w):
    bsz, n, _ = h1.shape
    tm = TOKEN_TILE
    tok = lambda width: pl.BlockSpec((1, tm, width), lambda b, i: (b, i, 0))
    return pl.pallas_call(
        _kernel_b,
        grid=(bsz, n // tm),
        in_specs=[tok(D_MODEL), tok(GLA_V_W), tok(GLA_V_W), tok(ATTN_Q_W), tok(P_DIM),
                  _const_spec((1, GLA_DV)),
                  _const_spec((GLA_V_W, D_MODEL)), _const_spec((ATTN_Q_W, D_MODEL)),
                  _const_spec((1, D_MODEL)), _const_spec((1, D_MODEL)),
                  _const_spec((D_MODEL, D_FF)), _const_spec((D_MODEL, D_FF)), _const_spec((D_FF, D_MODEL)),
                  _const_spec((1, D_MODEL)), _const_spec((1, D_MODEL)),
                  _const_spec((D_MODEL, D_MODEL)), _const_spec((1, D_MODEL)), _const_spec((P_DIM, D_MODEL))],
        out_specs=tok(D_MODEL),
        out_shape=jax.ShapeDtypeStruct((bsz, n, D_MODEL), F32),
        compiler_params=pltpu.CompilerParams(dimension_semantics=("arbitrary", "arbitrary"),
                                             vmem_limit_bytes=VMEM_LIMIT),
        name="outproj_ffn2_embed",
    )(h1, o_gla, gr, o_att, p, w["gn_g"], w["w_out_gla"], w["w_out_att"], w["ln2_g"], w["ln2_b"],
      w["ffn2_wg"], w["ffn2_wu"], w["ffn2_wd"], w["ln3_g"], w["ln3_b"], w["w_pg"], w["b_pg"], w["w_pe"])


def _rope_tables(n):
    t = jnp.arange(n, dtype=jnp.int32)
    row = (t // GRID_W).astype(F32)
    col = (t % GRID_W).astype(F32)
    axis_dim = ATTN_DH // 2
    inv_freq = ROPE_THETA ** (-jnp.arange(0, axis_dim, 2, dtype=F32) / axis_dim)
    lane = jnp.arange(LANES, dtype=jnp.int32) % ATTN_DH
    freq = inv_freq[lane % ROPE_HALF]
    pos = jnp.where((lane // axis_dim)[None, :] == 0, row[:, None], col[:, None])
    ang = pos * freq[None, :]
    first_half = ((lane % axis_dim) // ROPE_HALF == 0)[None, :]
    cos, sin = jnp.cos(ang), jnp.sin(ang)
    return cos, jnp.where(first_half, -sin, 0.0), jnp.where(first_half, 0.0, sin)


def _prepare(ffn1_wg, ffn1_wu, ffn1_wd, ln1_g, ln1_b, w_in, gla_w2f, gla_b2f, gla_w2b, gla_b2b,
             gla_gn_g, q_norm_g, k_norm_g, w_out, ln2_g, ln2_b, ffn2_wg, ffn2_wu, ffn2_wd,
             ln3_g, ln3_b, w_pg, b_pg, w_pe):
    row = lambda v: v.reshape(1, -1).astype(F32)
    z0 = 2 * GLA_QK_W + 2 * GLA_V_W
    w_in_r = jnp.concatenate([w_in[:, :z0], w_in[:, z0 + Z_W:], w_in[:, z0:z0 + Z_W]], axis=1)
    zeros = jnp.zeros((GLA_GATE_RANK, GLA_QK_W), F32)
    idx = jnp.arange(AVG_W)
    avg =jnp.where((idx[:, None] // ATTN_DH) == (idx[None, :] // ATTN_DH), 1.0 / ATTN_DH, 0.0)
    ci = jnp.arange(CUM_TILE)
    same_chunk = (ci[:, None] // GLA_CHUNK) == (ci[None, :] // GLA_CHUNK)
    lower = jnp.where(same_chunk & (ci[None, :] <= ci[:, None]), 1.0, 0.0)
    return {
        "ffn1_wg": ffn1_wg.astype(BF16), "ffn1_wu": ffn1_wu.astype(BF16), "ffn1_wd": ffn1_wd.astype(BF16),
        "ln1_g": row(ln1_g), "ln1_b": row(ln1_b),
        "w_in": w_in_r.astype(BF16), "avg": avg.astype(BF16),
        "q_gain": row(jnp.tile(q_norm_g, ATTN_HEADS)), "k_gain": row(jnp.tile(k_norm_g, ATTN_KV_HEADS)),
        "w2f": jnp.concatenate([gla_w2f, zeros], axis=0).astype(BF16), "b2f": row(gla_b2f),
        "w2b": jnp.concatenate([zeros, gla_w2b], axis=0).astype(BF16), "b2b": row(gla_b2b),
        "tri_lower": lower.astype(BF16), "tri_upper": lower.T.astype(BF16),
        "gn_g": row(gla_gn_g),
        "w_out_gla": w_out[:GLA_V_W].astype(BF16), "w_out_att": w_out[GLA_V_W:].astype(BF16),
        "ln2_g": row(ln2_g), "ln2_b": row(ln2_b),
        "ffn2_wg": ffn2_wg.astype(BF16), "ffn2_wu": ffn2_wu.astype(BF16), "ffn2_wd": ffn2_wd.astype(BF16),
        "ln3_g": row(ln3_g), "ln3_b": row(ln3_b),
        "w_pg": w_pg.astype(BF16), "b_pg": row(b_pg), "w_pe": w_pe.astype(BF16),
    }


def _encoder_layer(x, p, w):
    n = x.shape[1]
    assert n % GLA_TILE == 0 and n % ATTN_Q_TILE == 0 and n % TOKEN_TILE == 0 and n % GRID_W == 0
    h1, gqk, gv, gr, z, q, k, v = _call_a(x, w, _rope_tables(n))
    o_gla = _call_gla(gqk, gv, z, w)
    o_att = _call_attn(q, k, v)
    return _call_b(h1, o_gla, gr, o_att, p, w)


def kernel(x_prompt, x_sample, p_prompt, p_sample, ffn1_wg, ffn1_wu, ffn1_wd, ln1_g, ln1_b, w_in,
           gla_w2f, gla_b2f, gla_w2b, gla_b2b, gla_gn_g, q_norm_g, k_norm_g, w_out, ln2_g, ln2_b,
           ffn2_wg, ffn2_wu, ffn2_wd, ln3_g, ln3_b, w_pg, b_pg, w_pe):
    y_prompt, y_sample = x_prompt, x_sample
    for i in range(DEPTH):
        w = _prepare(ffn1_wg[i], ffn1_wu[i], ffn1_wd[i], ln1_g[i], ln1_b[i], w_in[i],
                     gla_w2f[i], gla_b2f[i], gla_w2b[i], gla_b2b[i], gla_gn_g[i], q_norm_g[i],
                     k_norm_g[i], w_out[i], ln2_g[i], ln2_b[i], ffn2_wg[i], ffn2_wu[i], ffn2_wd[i],
                     ln3_g[i], ln3_b[i], w_pg[i], b_pg[i], w_pe[i])
        y_prompt = _encoder_layer(y_prompt, p_prompt[i], w)
        y_sample = _encoder_layer(y_sample, p_sample[i], w)
    return (y_prompt, y_sample)
```

```python
import functools

import jax
import jax.numpy as jnp
from jax import lax
from jax.experimental import pallas as pl
from jax.experimental.pallas import tpu as pltpu

F32 = jnp.float32
BF16 = jnp.bfloat16

D_MODEL = 1024
D_FF = 2816
P_DIM = 256
GRID_W = 64
DEPTH = 1
GLA_HEADS = 4
GLA_DK = 64
GLA_DV = 128
GLA_GATE_RANK = 16
GLA_TAU = 16.0
GLA_CHUNK = 64
ATTN_HEADS = 8
ATTN_KV_HEADS = 2
ATTN_DH = 64
ROPE_THETA = 10000.0
LN_EPS = 1e-5
QK_EPS = 1e-6
GN_EPS = 1e-5
DEEPNORM_ALPHA = (2.0 * DEPTH) ** 0.25

GLA_QK_W = GLA_HEADS * GLA_DK
GLA_V_W = GLA_HEADS * GLA_DV
ATTN_Q_W = ATTN_HEADS * ATTN_DH
ATTN_KV_W = ATTN_KV_HEADS * ATTN_DH
Z_W = 2 * GLA_GATE_RANK
OFF_GQK = 0
OFF_GV = OFF_GQK + 2 * GLA_QK_W
OFF_GR = OFF_GV + GLA_V_W
OFF_AQ = OFF_GR + GLA_V_W
OFF_AK = OFF_AQ + ATTN_Q_W
OFF_AV = OFF_AK + ATTN_KV_W
OFF_Z = OFF_AV + ATTN_KV_W
D_IN = OFF_Z + Z_W

LANES = 128
AVG_W = 256
ROPE_HALF = ATTN_DH // 4
LOG2E = 1.4426950408889634
QK_SCALE = ATTN_DH ** -0.5 * LOG2E
GLA_SCALE = GLA_DK ** -0.5

TOKEN_TILE = 512
FF_CHUNKS = ((0, 1536), (1536, D_FF))
GLA_TILE = 1024
CUM_TILE = 256
ATTN_Q_TILE = 512
ATTN_UNIT_ROWS = 256
ATTN_UNIT_HEADS = 4
ATTN_KEY_TILE = 256
VT_ROWS = ATTN_DH + 16
VMEM_LIMIT = 56 * 1024 * 1024


def _const_spec(shape):
    return pl.BlockSpec(shape, lambda *_: (0,) * len(shape), pipeline_mode=pl.Buffered(1))


def _dot(a, b):
    return jnp.dot(a, b, preferred_element_type=F32)


def _dot_nt(a, b):
    return lax.dot_general(a, b, (((1,), (1,)), ((), ())), preferred_element_type=F32)


def _split_bf16(x):
    hi = x.astype(BF16)
    lo = (x - hi.astype(F32)).astype(BF16)
    return hi, lo


def _layer_norm(y, g, b):
    mu = jnp.mean(y, axis=-1, keepdims=True)
    d = y - mu
    var = jnp.mean(d * d, axis=-1, keepdims=True)
    return d * lax.rsqrt(var + LN_EPS) * g + b


def _swiglu(xb, wg_ref, wu_ref, wd_ref):
    out = None
    for lo, hi in FF_CHUNKS:
        g = _dot(xb, wg_ref[:, lo:hi])
        u = _dot(xb, wu_ref[:, lo:hi])
        hid = (g * jax.nn.sigmoid(g)) * u
        part = _dot(hid.astype(BF16), wd_ref[lo:hi, :])
        out = part if out is None else out + part
    return out


def _head_rms(x, avg, gain, eps):
    hi, lo = _split_bf16(x * x)
    w = min(x.shape[1], AVG_W)
    a = avg[:w, :w]
    ms = jnp.concatenate([_dot(hi[:, c:c + w], a) + _dot(lo[:, c:c + w], a) for c in range(0, x.shape[1], w)],
                         axis=1)
    return x * lax.rsqrt(ms + eps) * gain


def _rope(xs, c, sa, sb):
    return (xs * c + pltpu.roll(xs, LANES - ROPE_HALF, 1) * sa
            + pltpu.roll(xs, ROPE_HALF, 1) * sb)


def _attn_operands(att_ref, avg_ref, qg_ref, kg_ref, c_ref, sa_ref, sb_ref, q_ref, k_ref, v_ref):
    att = att_ref[...]
    c, sa, sb = c_ref[...], sa_ref[...], sb_ref[...]
    avg = avg_ref[...]
    qn = _head_rms(att[:, 0:ATTN_Q_W], avg, qg_ref[...], QK_EPS)
    for j in range(ATTN_Q_W // LANES):
        sl = slice(LANES * j, LANES * (j + 1))
        qr = _rope(qn[:, sl], c, sa, sb) * QK_SCALE
        for t in range(qr.shape[0] // ATTN_UNIT_ROWS):
            q_ref[0, t, sl, :] = qr[ATTN_UNIT_ROWS * t:ATTN_UNIT_ROWS * (t + 1), :].T.astype(BF16)
    kn = _head_rms(att[:, ATTN_Q_W:ATTN_Q_W + ATTN_KV_W], avg, kg_ref[...], QK_EPS)
    kr = _rope(kn, c, sa, sb).astype(BF16)
    vt = att[:, ATTN_Q_W + ATTN_KV_W:].T.astype(BF16)
    ones = jnp.ones((VT_ROWS - ATTN_DH, vt.shape[1]), BF16)
    for g in range(ATTN_KV_HEADS):
        k_ref[0, g] = kr[:, ATTN_DH * g:ATTN_DH * (g + 1)]
        v_ref[0, g, 0:ATTN_DH, :] = vt[ATTN_DH * g:ATTN_DH * (g + 1), :]
        v_ref[0, g, ATTN_DH:VT_ROWS, :] = ones


def _kernel_a(x_ref, wg_ref, wu_ref, wd_ref, lng_ref, lnb_ref, win_ref, avg_ref, qg_ref, kg_ref,
              c_ref, sa_ref, sb_ref,
              h_ref, gqk_ref, gv_ref, gr_ref, z_ref, q_ref, k_ref, v_ref, att_ref):
    i = pl.program_id(1)
    n_tiles = pl.num_programs(1) - 1
    operands = functools.partial(_attn_operands, att_ref, avg_ref, qg_ref, kg_ref, c_ref, sa_ref, sb_ref,
                                 q_ref, k_ref, v_ref)

    @pl.when(i == 0)
    def _():
        att_ref[...] = jnp.zeros_like(att_ref)

    @pl.when(i < n_tiles)
    def _():
        operands()
        x = x_ref[0]
        f = _swiglu(x.astype(BF16), wg_ref, wu_ref, wd_ref)
        h = _layer_norm(DEEPNORM_ALPHA * x + 0.5 * f, lng_ref[...], lnb_ref[...])
        h_ref[0] = h
        proj = _dot(h.astype(BF16), win_ref[...])
        gqk_ref[0] = proj[:, OFF_GQK:OFF_GV]
        gv_ref[0] = proj[:, OFF_GV:OFF_GR].astype(BF16)
        gr_ref[0] = proj[:, OFF_GR:OFF_AQ]
        z_ref[0] = proj[:, OFF_Z:D_IN]
        att_ref[...] = proj[:, OFF_AQ:OFF_Z]

    pl.when(i == n_tiles)(operands)


def _call_a(x, w, tabs):
    bsz, n, _ = x.shape
    tm = TOKEN_TILE
    nt = n // tm
    cur = lambda i: jnp.minimum(i, nt - 1)
    prev = lambda i: jnp.maximum(i - 1, 0)
    tok = lambda width: pl.BlockSpec((1, tm, width), lambda b, i: (b, cur(i), 0))
    tab = pl.BlockSpec((tm, LANES), lambda b, i: (prev(i), 0))
    kv_spec = pl.BlockSpec((1, ATTN_KV_HEADS, tm, ATTN_DH), lambda b, i: (b, 0, prev(i), 0))
    vt_spec = pl.BlockSpec((1, ATTN_KV_HEADS, VT_ROWS, tm), lambda b, i: (b, 0, 0, prev(i)))
    qt_spec = pl.BlockSpec((1, tm // ATTN_UNIT_ROWS, ATTN_Q_W, ATTN_UNIT_ROWS), lambda b, i: (b, prev(i), 0, 0))
    out_shape = (
        jax.ShapeDtypeStruct((bsz, n, D_MODEL), F32),
        jax.ShapeDtypeStruct((bsz, n, 2 * GLA_QK_W), F32),
        jax.ShapeDtypeStruct((bsz, n, GLA_V_W), BF16),
        jax.ShapeDtypeStruct((bsz, n, GLA_V_W), F32),
        jax.ShapeDtypeStruct((bsz, n, Z_W), F32),
        jax.ShapeDtypeStruct((bsz, n // ATTN_UNIT_ROWS, ATTN_Q_W, ATTN_UNIT_ROWS), BF16),
        jax.ShapeDtypeStruct((bsz, ATTN_KV_HEADS, n, ATTN_DH), BF16),
        jax.ShapeDtypeStruct((bsz, ATTN_KV_HEADS, VT_ROWS, n), BF16),
    )
    return pl.pallas_call(
        _kernel_a,
        grid=(bsz, nt + 1),
        in_specs=[tok(D_MODEL),
                  _const_spec((D_MODEL, D_FF)), _const_spec((D_MODEL, D_FF)), _const_spec((D_FF, D_MODEL)),
                  _const_spec((1, D_MODEL)), _const_spec((1, D_MODEL)),
                  _const_spec((D_MODEL, D_IN)), _const_spec((AVG_W, AVG_W)),
                  _const_spec((1, ATTN_Q_W)), _const_spec((1, ATTN_KV_W)),
                  tab, tab, tab],
        out_specs=(tok(D_MODEL), tok(2 * GLA_QK_W), tok(GLA_V_W), tok(GLA_V_W), tok(Z_W), qt_spec,
                   kv_spec, vt_spec),
        out_shape=out_shape,
        scratch_shapes=[pltpu.VMEM((tm, OFF_Z - OFF_AQ), F32)],
        compiler_params=pltpu.CompilerParams(dimension_semantics=("arbitrary", "arbitrary"),
                                             vmem_limit_bytes=VMEM_LIMIT),
        name="ffn1_inproj",
    )(x, w["ffn1_wg"], w["ffn1_wu"], w["ffn1_wd"], w["ln1_g"], w["ln1_b"], w["w_in"], w["avg"],
      w["q_gain"], w["k_gain"], *tabs)


def _gla_log_decay_cumsum(z_ref, w2_ref, b2_ref, tri_ref):
    tn = z_ref.shape[1]
    pre = _dot(z_ref[0].astype(BF16), w2_ref[...]) + b2_ref[...]
    log_a = (jnp.minimum(pre, 0.0) - jnp.log(1.0 + jnp.exp(-jnp.abs(pre)))) * (1.0 / GLA_TAU)
    hi, lo = _split_bf16(log_a)
    tri = tri_ref[...]
    cum = jnp.concatenate(
        [_dot(tri, hi[r:r + CUM_TILE]) + _dot(tri, lo[r:r + CUM_TILE]) for r in range(0, tn, CUM_TILE)],
        axis=0)
    return cum * LOG2E


def _gla_chunk(gqk_ref, gv_ref, out_ref, out_row0, accumulate, cum, state, c, half_mask, keep, reverse):
    c_len = GLA_CHUNK
    r0 = c_len * c
    cc = cum[r0:r0 + c_len]
    mid = cc[c_len // 2:c_len // 2 + 1] if reverse else cc[c_len // 2 - 1:c_len // 2]
    last = cc[0:1] if reverse else cc[c_len - 1:c_len]
    qc = gqk_ref[0, r0:r0 + c_len, 0:GLA_QK_W] * GLA_SCALE
    kc = gqk_ref[0, r0:r0 + c_len, GLA_QK_W:2 * GLA_QK_W]
    vc = gv_ref[0, r0:r0 + c_len, :]
    rel = cc - mid
    q_in = qc * jnp.exp2(rel)
    k_in = (kc * jnp.exp2(-rel)).astype(BF16)
    k_up = kc * jnp.exp2(last - cc)
    q_st = qc * jnp.exp2(cc)
    state_b = state.astype(BF16)
    scores, o_state = [], []
    for col in range(GLA_QK_W // LANES):
        lanes = slice(LANES * col, LANES * (col + 1))
        stack = lambda x: jnp.concatenate([x[:, lanes].astype(BF16) * m for m in half_mask], axis=0)
        scores.append(_dot_nt(stack(q_in), k_in[:, lanes]))
        o_state.append(_dot(stack(q_st), state_b[lanes, :]))
    p = jnp.where(keep, jnp.concatenate(scores, axis=0), 0.0).astype(BF16)
    o_state = jnp.concatenate(o_state, axis=0)
    xt = jnp.concatenate([k_up, jnp.broadcast_to(last, (c_len, GLA_QK_W))], axis=0).T
    k_up_t = xt[:, 0:c_len].astype(BF16)
    decay_col = jnp.exp2(xt[:, c_len:c_len + 1])
    upd = []
    for h in range(GLA_HEADS):
        rows = slice(c_len * h, c_len * (h + 1))
        vh = vc[:, GLA_DV * h:GLA_DV * (h + 1)]
        o_h = _dot(p[rows], vh) + o_state[rows]
        dst = (0, pl.ds(pl.multiple_of(out_row0 + r0, c_len), c_len), slice(GLA_DV * h, GLA_DV * (h + 1)))
        out_ref[dst] = out_ref[dst] + o_h if accumulate else o_h
        upd.append(_dot(k_up_t[rows], vh))
    return decay_col * state + jnp.concatenate(upd, axis=0)


def _kernel_gla(gqkf_ref, gvf_ref, zf_ref, gqkb_ref, gvb_ref, zb_ref,
                w2f_ref, b2f_ref, w2b_ref, b2b_ref, lower_ref, upper_ref,
                o_ref, state_ref):
    j = pl.program_id(1)
    nb = pl.num_programs(1)

    @pl.when(j == 0)
    def _():
        state_ref[...] = jnp.zeros_like(state_ref)

    tn = gqkf_ref.shape[1]
    pl.when(j < nb // 2)(functools.partial(
        _gla_step, gqkf_ref, gvf_ref, zf_ref, gqkb_ref, gvb_ref, zb_ref, w2f_ref, b2f_ref, w2b_ref, b2b_ref,
        lower_ref, upper_ref, o_ref, state_ref, j * tn, (nb - 1 - j) * tn, False))
    pl.when(j >= nb // 2)(functools.partial(
        _gla_step, gqkf_ref, gvf_ref, zf_ref, gqkb_ref, gvb_ref, zb_ref, w2f_ref, b2f_ref, w2b_ref, b2b_ref,
        lower_ref, upper_ref, o_ref, state_ref, j * tn, (nb - 1 - j) * tn, True))


def _gla_step(gqkf_ref, gvf_ref, zf_ref, gqkb_ref, gvb_ref, zb_ref, w2f_ref, b2f_ref, w2b_ref, b2b_ref,
              lower_ref, upper_ref, o_ref, state_ref, row_f, row_b, accumulate):
    c_len = GLA_CHUNK
    n_chunks = gqkf_ref.shape[1] // c_len
    cum_f = _gla_log_decay_cumsum(zf_ref, w2f_ref, b2f_ref, lower_ref)
    cum_b = _gla_log_decay_cumsum(zb_ref, w2b_ref, b2b_ref, upper_ref)
    first_half = lax.broadcasted_iota(jnp.int32, (c_len, LANES), 1) < GLA_DK
    half_mask = [first_half.astype(BF16), (~first_half).astype(BF16)]
    row = lax.broadcasted_iota(jnp.int32, (GLA_HEADS * c_len, c_len), 0) % c_len
    col = lax.broadcasted_iota(jnp.int32, (GLA_HEADS * c_len, c_len), 1)
    state_f, state_b = state_ref[0], state_ref[1]
    for c in range(n_chunks):
        state_f = _gla_chunk(gqkf_ref, gvf_ref, o_ref, row_f, accumulate, cum_f, state_f, c, half_mask,
                             col <= row, False)
        state_b = _gla_chunk(gqkb_ref, gvb_ref, o_ref, row_b, accumulate, cum_b, state_b, n_chunks - 1 - c,
                             half_mask, col > row, True)
    state_ref[0] = state_f
    state_ref[1] = state_b


def _call_gla(gqk, gv, z, w):
    bsz, n, _ = gqk.shape
    tn = GLA_TILE
    nb = n // tn
    fwd = lambda width: pl.BlockSpec((1, tn, width), lambda b, j: (b, j, 0))
    bwd = lambda width: pl.BlockSpec((1, tn, width), lambda b, j: (b, nb - 1 - j, 0))
    assert nb % 2 == 0
    return pl.pallas_call(
        _kernel_gla,
        grid=(bsz, nb),
        in_specs=[fwd(2 * GLA_QK_W), fwd(GLA_V_W), fwd(Z_W), bwd(2 * GLA_QK_W), bwd(GLA_V_W), bwd(Z_W),
                  _const_spec((Z_W, GLA_QK_W)), _const_spec((1, GLA_QK_W)),
                  _const_spec((Z_W, GLA_QK_W)), _const_spec((1, GLA_QK_W)),
                  _const_spec((CUM_TILE, CUM_TILE)), _const_spec((CUM_TILE, CUM_TILE))],
        out_specs=pl.BlockSpec((1, n, GLA_V_W), lambda b, j: (b, 0, 0)),
        out_shape=jax.ShapeDtypeStruct((bsz, n, GLA_V_W), F32),
        scratch_shapes=[pltpu.VMEM((2, GLA_HEADS * GLA_DK, GLA_DV), F32)],
        compiler_params=pltpu.CompilerParams(dimension_semantics=("arbitrary", "arbitrary"),
                                             vmem_limit_bytes=VMEM_LIMIT),
        name="gla_bidir",
    )(gqk, gv, z, gqk, gv, z, w["w2f"], w["b2f"], w["w2b"], w["b2b"], w["tri_lower"], w["tri_upper"])


def _kernel_attn(flag_ref, qt_ref, k_ref, vt_ref, o_ref, st_ref, m_ref):
    n = k_ref.shape[2]
    sub = ATTN_UNIT_ROWS
    hpu = ATTN_UNIT_HEADS
    width = hpu * sub
    groups = ATTN_HEADS // ATTN_KV_HEADS // hpu
    n_units = (ATTN_Q_TILE // sub) * groups
    tiles = [slice(t, t + ATTN_KEY_TILE) for t in range(0, n, ATTN_KEY_TILE)]
    step = pl.program_id(2)
    last_step = pl.num_programs(2) - 1

    def unit_window(at_step, u):
        s, hg = divmod(u, groups)
        return at_step * (ATTN_Q_TILE // sub) + s, hg

    def run_phase(u_scores, scores_step, u_pv):
        if u_scores is not None:
            tile_idx, hg = unit_window(scores_step, u_scores)
            heads = [qt_ref[0, tile_idx, ATTN_DH * h:ATTN_DH * (h + 1), :]
                     for h in range(hpu * hg, hpu * (hg + 1))]
            q_cur = jnp.concatenate(heads, axis=1)
            m_run = None
        if u_pv is not None:
            m_fin = m_ref[u_pv]
            acc = jnp.zeros((VT_ROWS, width), F32)
        for idx in range(len(tiles)):
            if u_pv is not None:
                tile = tiles[idx]
                pt = jnp.exp2(st_ref[u_pv % 2, tile, :] - m_fin).astype(BF16)
                acc = acc + _dot(vt_ref[0, 0, :, tile], pt)
            if u_scores is not None:
                tile = tiles[(idx + len(tiles) // 2) % len(tiles)]
                st = _dot(k_ref[0, 0, tile, :], q_cur)
                st_ref[u_scores % 2, tile, :] = st
                m_tile = jnp.max(st, axis=0, keepdims=True)
                m_run = m_tile if m_run is None else jnp.maximum(m_run, m_tile)
        if u_scores is not None:
            m_ref[u_scores] = m_run
        if u_pv is not None:
            tile_idx, hg = unit_window(step, u_pv)
            row0 = pl.multiple_of(tile_idx * sub, sub)
            o2 = (acc[0:ATTN_DH] / acc[ATTN_DH:ATTN_DH + 1]).T
            o_ref[0, pl.ds(row0, sub), ATTN_DH * hpu * hg:ATTN_DH * hpu * (hg + 1)] = (
                jnp.concatenate([o2[sub * i:sub * (i + 1)] for i in range(hpu)], axis=1).astype(BF16))

    pl.when(step == 0)(functools.partial(run_phase, 0, step, None))
    for u in range(n_units - 1):
        pl.when(flag_ref[u] == 0)(functools.partial(run_phase, u + 1, step, u))
    pl.when(step < last_step)(functools.partial(run_phase, 0, step + 1, n_units - 1))
    pl.when(step == last_step)(functools.partial(run_phase, None, None, n_units - 1))


def _call_attn(qt, k, vt):
    bsz, n = qt.shape[0], k.shape[2]
    group_w = ATTN_Q_W // ATTN_KV_HEADS
    width = ATTN_UNIT_HEADS * ATTN_UNIT_ROWS
    n_units = (ATTN_Q_TILE // ATTN_UNIT_ROWS) * (group_w // (ATTN_UNIT_HEADS * ATTN_DH))
    assert n_units % 2 == 0
    qt_spec = pl.BlockSpec((1, n // ATTN_UNIT_ROWS, group_w, ATTN_UNIT_ROWS), lambda b, g, i: (b, 0, g, 0))
    o_spec = pl.BlockSpec((1, n, group_w), lambda b, g, i: (b, 0, g))
    k_spec = pl.BlockSpec((1, 1, n, ATTN_DH), lambda b, g, i: (b, g, 0, 0))
    vt_spec = pl.BlockSpec((1, 1, VT_ROWS, n), lambda b, g, i: (b, g, 0, 0))
    return pl.pallas_call(
        _kernel_attn,
        grid=(bsz, ATTN_KV_HEADS, n // ATTN_Q_TILE),
        in_specs=[pl.BlockSpec(memory_space=pltpu.SMEM), qt_spec, k_spec, vt_spec],
        out_specs=o_spec,
        out_shape=jax.ShapeDtypeStruct((bsz, n, ATTN_Q_W), BF16),
        scratch_shapes=[pltpu.VMEM((2, n, width), F32),
                        pltpu.VMEM((n_units, 1, width), F32)],
        compiler_params=pltpu.CompilerParams(
            dimension_semantics=("arbitrary", "arbitrary", "arbitrary"),
            vmem_limit_bytes=VMEM_LIMIT),
        name="gqa_attention",
    )(jnp.zeros((n_units,), jnp.int32), qt, k, vt)


def _kernel_b(h1_ref, og_ref, gr_ref, oa_ref, p_ref, gn_ref, wog_ref, woa_ref,
              ln2g_ref, ln2b_ref, wg_ref, wu_ref, wd_ref, ln3g_ref, ln3b_ref, wpg_ref, bpg_ref, wpe_ref,
              out_ref):
    o = og_ref[0]
    gn = gn_ref[...]
    normed = []
    for h in range(GLA_HEADS):
        oh = o[:, GLA_DV * h:GLA_DV * (h + 1)]
        ms = jnp.mean(oh * oh, axis=-1, keepdims=True)
        normed.append(oh * lax.rsqrt(ms + GN_EPS) * gn)
    gr = gr_ref[0]
    o_gla = jnp.concatenate(normed, axis=1) * (gr * jax.nn.sigmoid(gr))
    mix = _dot(o_gla.astype(BF16), wog_ref[...]) + _dot(oa_ref[0], woa_ref[...])
    h2 = _layer_norm(DEEPNORM_ALPHA * h1_ref[0] + mix, ln2g_ref[...], ln2b_ref[...])
    f = _swiglu(h2.astype(BF16), wg_ref, wu_ref, wd_ref)
    h3 = _layer_norm(DEEPNORM_ALPHA * h2 + 0.5 * f, ln3g_ref[...], ln3b_ref[...])
    gate = jax.nn.sigmoid(_dot(h3.astype(BF16), wpg_ref[...]) + bpg_ref[...])
    out_ref[0] = h3 + gate * _dot(p_ref[0].astype(BF16), wpe_ref[...])


def _call_b(h1, o_gla, gr, o_att, p, w):
    bsz, n, _ = h1.shape
    tm = TOKEN_TILE
    tok = lambda width: pl.BlockSpec((1, tm, width), lambda b, i: (b, i, 0))
    return pl.pallas_call(
        _kernel_b,
        grid=(bsz, n // tm),
        in_specs=[tok(D_MODEL), tok(GLA_V_W), tok(GLA_V_W), tok(ATTN_Q_W), tok(P_DIM),
                  _const_spec((1, GLA_DV)),
                  _const_spec((GLA_V_W, D_MODEL)), _const_spec((ATTN_Q_W, D_MODEL)),
                  _const_spec((1, D_MODEL)), _const_spec((1, D_MODEL)),
                  _const_spec((D_MODEL, D_FF)), _const_spec((D_MODEL, D_FF)), _const_spec((D_FF, D_MODEL)),
                  _const_spec((1, D_MODEL)), _const_spec((1, D_MODEL)),
                  _const_spec((D_MODEL, D_MODEL)), _const_spec((1, D_MODEL)), _const_spec((P_DIM, D_MODEL))],
        out_specs=tok(D_MODEL),
        out_shape=jax.ShapeDtypeStruct((bsz, n, D_MODEL), F32),
        compiler_params=pltpu.CompilerParams(dimension_semantics=("arbitrary", "arbitrary"),
                                             vmem_limit_bytes=VMEM_LIMIT),
        name="outproj_ffn2_embed",
    )(h1, o_gla, gr, o_att, p, w["gn_g"], w["w_out_gla"], w["w_out_att"], w["ln2_g"], w["ln2_b"],
      w["ffn2_wg"], w["ffn2_wu"], w["ffn2_wd"], w["ln3_g"], w["ln3_b"], w["w_pg"], w["b_pg"], w["w_pe"])


def _rope_tables(n):
    t = jnp.arange(n, dtype=jnp.int32)
    row = (t // GRID_W).astype(F32)
    col = (t % GRID_W).astype(F32)
    axis_dim = ATTN_DH // 2
    inv_freq = ROPE_THETA ** (-jnp.arange(0, axis_dim, 2, dtype=F32) / axis_dim)
    lane = jnp.arange(LANES, dtype=jnp.int32) % ATTN_DH
    freq = inv_freq[lane % ROPE_HALF]
    pos = jnp.where((lane // axis_dim)[None, :] == 0, row[:, None], col[:, None])
    ang = pos * freq[None, :]
    first_half = ((lane % axis_dim) // ROPE_HALF == 0)[None, :]
    cos, sin = jnp.cos(ang), jnp.sin(ang)
    return cos, jnp.where(first_half, -sin, 0.0), jnp.where(first_half, 0.0, sin)


def _prepare(ffn1_wg, ffn1_wu, ffn1_wd, ln1_g, ln1_b, w_in, gla_w2f, gla_b2f, gla_w2b, gla_b2b,
             gla_gn_g, q_norm_g, k_norm_g, w_out, ln2_g, ln2_b, ffn2_wg, ffn2_wu, ffn2_wd,
             ln3_g, ln3_b, w_pg, b_pg, w_pe):
    row = lambda v: v.reshape(1, -1).astype(F32)
    z0 = 2 * GLA_QK_W + 2 * GLA_V_W
    w_in_r = jnp.concatenate([w_in[:, :z0], w_in[:, z0 + Z_W:], w_in[:, z0:z0 + Z_W]], axis=1)
    zeros = jnp.zeros((GLA_GATE_RANK, GLA_QK_W), F32)
    idx = jnp.arange(AVG_W)
    avg =jnp.where((idx[:, None] // ATTN_DH) == (idx[None, :] // ATTN_DH), 1.0 / ATTN_DH, 0.0)
    ci = jnp.arange(CUM_TILE)
    same_chunk = (ci[:, None] // GLA_CHUNK) == (ci[None, :] // GLA_CHUNK)
    lower = jnp.where(same_chunk & (ci[None, :] <= ci[:, None]), 1.0, 0.0)
    return {
        "ffn1_wg": ffn1_wg.astype(BF16), "ffn1_wu": ffn1_wu.astype(BF16), "ffn1_wd": ffn1_wd.astype(BF16),
        "ln1_g": row(ln1_g), "ln1_b": row(ln1_b),
        "w_in": w_in_r.astype(BF16), "avg": avg.astype(BF16),
        "q_gain": row(jnp.tile(q_norm_g, ATTN_HEADS)), "k_gain": row(jnp.tile(k_norm_g, ATTN_KV_HEADS)),
        "w2f": jnp.concatenate([gla_w2f, zeros], axis=0).astype(BF16), "b2f": row(gla_b2f),
        "w2b": jnp.concatenate([zeros, gla_w2b], axis=0).astype(BF16), "b2b": row(gla_b2b),
        "tri_lower": lower.astype(BF16), "tri_upper": lower.T.astype(BF16),
        "gn_g": row(gla_gn_g),
        "w_out_gla": w_out[:GLA_V_W].astype(BF16), "w_out_att": w_out[GLA_V_W:].astype(BF16),
        "ln2_g": row(ln2_g), "ln2_b": row(ln2_b),
        "ffn2_wg": ffn2_wg.astype(BF16), "ffn2_wu": ffn2_wu.astype(BF16), "ffn2_wd": ffn2_wd.astype(BF16),
        "ln3_g": row(ln3_g), "ln3_b": row(ln3_b),
        "w_pg": w_pg.astype(BF16), "b_pg": row(b_pg), "w_pe": w_pe.astype(BF16),
    }


def _encoder_layer(x, p, w):
    n = x.shape[1]
    assert n % GLA_TILE == 0 and n % ATTN_Q_TILE == 0 and n % TOKEN_TILE == 0 and n % GRID_W == 0
    h1, gqk, gv, gr, z, q, k, v = _call_a(x, w, _rope_tables(n))
    o_gla = _call_gla(gqk, gv, z, w)
    o_att = _call_attn(q, k, v)
    return _call_b(h1, o_gla, gr, o_att, p, w)


def kernel(x_prompt, x_sample, p_prompt, p_sample, ffn1_wg, ffn1_wu, ffn1_wd, ln1_g, ln1_b, w_in,
           gla_w2f, gla_b2f, gla_w2b, gla_b2b, gla_gn_g, q_norm_g, k_norm_g, w_out, ln2_g, ln2_b,
           ffn2_wg, ffn2_wu, ffn2_wd, ln3_g, ln3_b, w_pg, b_pg, w_pe):
    y_prompt, y_sample = x_prompt, x_sample
    for i in range(DEPTH):
        w = _prepare(ffn1_wg[i], ffn1_wu[i], ffn1_wd[i], ln1_g[i], ln1_b[i], w_in[i],
                     gla_w2f[i], gla_b2f[i], gla_w2b[i], gla_b2b[i], gla_gn_g[i], q_norm_g[i],
                     k_norm_g[i], w_out[i], ln2_g[i], ln2_b[i], ffn2_wg[i], ffn2_wu[i], ffn2_wd[i],
                     ln3_g[i], ln3_b[i], w_pg[i], b_pg[i], w_pe[i])
        y_prompt = _encoder_layer(y_prompt, p_prompt[i], w)
        y_sample = _encoder_layer(y_sample, p_sample[i], w)
    return (y_prompt, y_sample)
```

```python
import functools

import jax
import jax.numpy as jnp
from jax import lax
from jax.experimental import pallas as pl
from jax.experimental.pallas import tpu as pltpu

F32 = jnp.float32
BF16 = jnp.bfloat16

D_MODEL = 1024
D_FF = 2816
P_DIM = 256
GRID_W = 64
DEPTH = 1
GLA_HEADS = 4
GLA_DK = 64
GLA_DV = 128
GLA_GATE_RANK = 16
GLA_TAU = 16.0
GLA_CHUNK = 64
ATTN_HEADS = 8
ATTN_KV_HEADS = 2
ATTN_DH = 64
ROPE_THETA = 10000.0
LN_EPS = 1e-5
QK_EPS = 1e-6
GN_EPS = 1e-5
DEEPNORM_ALPHA = (2.0 * DEPTH) ** 0.25

GLA_QK_W = GLA_HEADS * GLA_DK
GLA_V_W = GLA_HEADS * GLA_DV
ATTN_Q_W = ATTN_HEADS * ATTN_DH
ATTN_KV_W = ATTN_KV_HEADS * ATTN_DH
Z_W = 2 * GLA_GATE_RANK
OFF_GQK = 0
OFF_GV = OFF_GQK + 2 * GLA_QK_W
OFF_GR = OFF_GV + GLA_V_W
OFF_AQ = OFF_GR + GLA_V_W
OFF_AK = OFF_AQ + ATTN_Q_W
OFF_AV = OFF_AK + ATTN_KV_W
OFF_Z = OFF_AV + ATTN_KV_W
D_IN = OFF_Z + Z_W

LANES = 128
AVG_W = 256
ROPE_HALF = ATTN_DH // 4
LOG2E = 1.4426950408889634
QK_SCALE = ATTN_DH ** -0.5 * LOG2E
GLA_SCALE = GLA_DK ** -0.5

TOKEN_TILE = 512
FF_CHUNKS = ((0, 1536), (1536, D_FF))
GLA_TILE = 1024
CUM_TILE = 256
ATTN_Q_TILE = 1024
ATTN_UNIT_ROWS = 256
ATTN_UNIT_HEADS = 2
ATTN_KEY_TILE = 256
VT_ROWS = ATTN_DH + 16
VMEM_LIMIT = 56 * 1024 * 1024


def _const_spec(shape):
    return pl.BlockSpec(shape, lambda *_: (0,) * len(shape), pipeline_mode=pl.Buffered(1))


def _dot(a, b):
    return jnp.dot(a, b, preferred_element_type=F32)


def _dot_nt(a, b):
    return lax.dot_general(a, b, (((1,), (1,)), ((), ())), preferred_element_type=F32)


def _split_bf16(x):
    hi = x.astype(BF16)
    lo = (x - hi.astype(F32)).astype(BF16)
    return hi, lo


def _layer_norm(y, g, b):
    mu = jnp.mean(y, axis=-1, keepdims=True)
    d = y - mu
    var = jnp.mean(d * d, axis=-1, keepdims=True)
    return d * lax.rsqrt(var + LN_EPS) * g + b


def _swiglu(xb, wg_ref, wu_ref, wd_ref):
    out = None
    for lo, hi in FF_CHUNKS:
        g = _dot(xb, wg_ref[:, lo:hi])
        u = _dot(xb, wu_ref[:, lo:hi])
        hid = (g * jax.nn.sigmoid(g)) * u
        part = _dot(hid.astype(BF16), wd_ref[lo:hi, :])
        out = part if out is None else out + part
    return out


def _head_rms(x, avg, gain, eps):
    hi, lo = _split_bf16(x * x)
    w = min(x.shape[1], AVG_W)
    a = avg[:w, :w]
    ms = jnp.concatenate([_dot(hi[:, c:c + w], a) + _dot(lo[:, c:c + w], a) for c in range(0, x.shape[1], w)],
                         axis=1)
    return x * lax.rsqrt(ms + eps) * gain


def _rope(xs, c, sa, sb):
    return (xs * c + pltpu.roll(xs, LANES - ROPE_HALF, 1) * sa
            + pltpu.roll(xs, ROPE_HALF, 1) * sb)


def _attn_operands(att_ref, avg_ref, qg_ref, kg_ref, c_ref, sa_ref, sb_ref, q_ref, k_ref, v_ref):
    att = att_ref[...]
    c, sa, sb = c_ref[...], sa_ref[...], sb_ref[...]
    avg = avg_ref[...]
    qn = _head_rms(att[:, 0:ATTN_Q_W], avg, qg_ref[...], QK_EPS)
    for j in range(ATTN_Q_W // LANES):
        sl = slice(LANES * j, LANES * (j + 1))
        qr = _rope(qn[:, sl], c, sa, sb) * QK_SCALE
        for t in range(qr.shape[0] // ATTN_UNIT_ROWS):
            q_ref[0, t, sl, :] = qr[ATTN_UNIT_ROWS * t:ATTN_UNIT_ROWS * (t + 1), :].T.astype(BF16)
    kn = _head_rms(att[:, ATTN_Q_W:ATTN_Q_W + ATTN_KV_W], avg, kg_ref[...], QK_EPS)
    kr = _rope(kn, c, sa, sb).astype(BF16)
    vt = att[:, ATTN_Q_W + ATTN_KV_W:].T.astype(BF16)
    ones = jnp.ones((VT_ROWS - ATTN_DH, vt.shape[1]), BF16)
    for g in range(ATTN_KV_HEADS):
        k_ref[0, g] = kr[:, ATTN_DH * g:ATTN_DH * (g + 1)]
        v_ref[0, g, 0:ATTN_DH, :] = vt[ATTN_DH * g:ATTN_DH * (g + 1), :]
        v_ref[0, g, ATTN_DH:VT_ROWS, :] = ones


def _kernel_a(x_ref, wg_ref, wu_ref, wd_ref, lng_ref, lnb_ref, win_ref, avg_ref, qg_ref, kg_ref,
              c_ref, sa_ref, sb_ref,
              h_ref, gqk_ref, gv_ref, gr_ref, z_ref, q_ref, k_ref, v_ref, att_ref):
    i = pl.program_id(1)
    n_tiles = pl.num_programs(1) - 1
    operands = functools.partial(_attn_operands, att_ref, avg_ref, qg_ref, kg_ref, c_ref, sa_ref, sb_ref,
                                 q_ref, k_ref, v_ref)

    @pl.when(i == 0)
    def _():
        att_ref[...] = jnp.zeros_like(att_ref)

    @pl.when(i < n_tiles)
    def _():
        operands()
        x = x_ref[0]
        f = _swiglu(x.astype(BF16), wg_ref, wu_ref, wd_ref)
        h = _layer_norm(DEEPNORM_ALPHA * x + 0.5 * f, lng_ref[...], lnb_ref[...])
        h_ref[0] = h
        proj = _dot(h.astype(BF16), win_ref[...])
        gqk_ref[0] = proj[:, OFF_GQK:OFF_GV]
        gv_ref[0] = proj[:, OFF_GV:OFF_GR].astype(BF16)
        gr_ref[0] = proj[:, OFF_GR:OFF_AQ]
        z_ref[0] = proj[:, OFF_Z:D_IN]
        att_ref[...] = proj[:, OFF_AQ:OFF_Z]

    pl.when(i == n_tiles)(operands)


def _call_a(x, w, tabs):
    bsz, n, _ = x.shape
    tm = TOKEN_TILE
    nt = n // tm
    cur = lambda i: jnp.minimum(i, nt - 1)
    prev = lambda i: jnp.maximum(i - 1, 0)
    tok = lambda width: pl.BlockSpec((1, tm, width), lambda b, i: (b, cur(i), 0))
    tab = pl.BlockSpec((tm, LANES), lambda b, i: (prev(i), 0))
    kv_spec = pl.BlockSpec((1, ATTN_KV_HEADS, tm, ATTN_DH), lambda b, i: (b, 0, prev(i), 0))
    vt_spec = pl.BlockSpec((1, ATTN_KV_HEADS, VT_ROWS, tm), lambda b, i: (b, 0, 0, prev(i)))
    qt_spec = pl.BlockSpec((1, tm // ATTN_UNIT_ROWS, ATTN_Q_W, ATTN_UNIT_ROWS), lambda b, i: (b, prev(i), 0, 0))
    out_shape = (
        jax.ShapeDtypeStruct((bsz, n, D_MODEL), F32),
        jax.ShapeDtypeStruct((bsz, n, 2 * GLA_QK_W), F32),
        jax.ShapeDtypeStruct((bsz, n, GLA_V_W), BF16),
        jax.ShapeDtypeStruct((bsz, n, GLA_V_W), F32),
        jax.ShapeDtypeStruct((bsz, n, Z_W), F32),
        jax.ShapeDtypeStruct((bsz, n // ATTN_UNIT_ROWS, ATTN_Q_W, ATTN_UNIT_ROWS), BF16),
        jax.ShapeDtypeStruct((bsz, ATTN_KV_HEADS, n, ATTN_DH), BF16),
        jax.ShapeDtypeStruct((bsz, ATTN_KV_HEADS, VT_ROWS, n), BF16),
    )
    return pl.pallas_call(
        _kernel_a,
        grid=(bsz, nt + 1),
        in_specs=[tok(D_MODEL),
                  _const_spec((D_MODEL, D_FF)), _const_spec((D_MODEL, D_FF)), _const_spec((D_FF, D_MODEL)),
                  _const_spec((1, D_MODEL)), _const_spec((1, D_MODEL)),
                  _const_spec((D_MODEL, D_IN)), _const_spec((AVG_W, AVG_W)),
                  _const_spec((1, ATTN_Q_W)), _const_spec((1, ATTN_KV_W)),
                  tab, tab, tab],
        out_specs=(tok(D_MODEL), tok(2 * GLA_QK_W), tok(GLA_V_W), tok(GLA_V_W), tok(Z_W), qt_spec,
                   kv_spec, vt_spec),
        out_shape=out_shape,
        scratch_shapes=[pltpu.VMEM((tm, OFF_Z - OFF_AQ), F32)],
        compiler_params=pltpu.CompilerParams(dimension_semantics=("arbitrary", "arbitrary"),
                                             vmem_limit_bytes=VMEM_LIMIT),
        name="ffn1_inproj",
    )(x, w["ffn1_wg"], w["ffn1_wu"], w["ffn1_wd"], w["ln1_g"], w["ln1_b"], w["w_in"], w["avg"],
      w["q_gain"], w["k_gain"], *tabs)


def _gla_log_decay_cumsum(z_ref, w2_ref, b2_ref, tri_ref):
    tn = z_ref.shape[1]
    pre = _dot(z_ref[0].astype(BF16), w2_ref[...]) + b2_ref[...]
    log_a = (jnp.minimum(pre, 0.0) - jnp.log(1.0 + jnp.exp(-jnp.abs(pre)))) * (1.0 / GLA_TAU)
    hi, lo = _split_bf16(log_a)
    tri = tri_ref[...]
    cum = jnp.concatenate(
        [_dot(tri, hi[r:r + CUM_TILE]) + _dot(tri, lo[r:r + CUM_TILE]) for r in range(0, tn, CUM_TILE)],
        axis=0)
    return cum * LOG2E


def _gla_chunk(gqk_ref, gv_ref, out_ref, out_row0, accumulate, cum, state, c, half_mask, keep, reverse):
    c_len = GLA_CHUNK
    r0 = c_len * c
    cc = cum[r0:r0 + c_len]
    mid = cc[c_len // 2:c_len // 2 + 1] if reverse else cc[c_len // 2 - 1:c_len // 2]
    last = cc[0:1] if reverse else cc[c_len - 1:c_len]
    qc = gqk_ref[0, r0:r0 + c_len, 0:GLA_QK_W] * GLA_SCALE
    kc = gqk_ref[0, r0:r0 + c_len, GLA_QK_W:2 * GLA_QK_W]
    vc = gv_ref[0, r0:r0 + c_len, :]
    rel = cc - mid
    q_in = qc * jnp.exp2(rel)
    k_in = (kc * jnp.exp2(-rel)).astype(BF16)
    k_up = kc * jnp.exp2(last - cc)
    q_st = qc * jnp.exp2(cc)
    state_b = state.astype(BF16)
    scores, o_state = [], []
    for col in range(GLA_QK_W // LANES):
        lanes = slice(LANES * col, LANES * (col + 1))
        stack = lambda x: jnp.concatenate([x[:, lanes].astype(BF16) * m for m in half_mask], axis=0)
        scores.append(_dot_nt(stack(q_in), k_in[:, lanes]))
        o_state.append(_dot(stack(q_st), state_b[lanes, :]))
    p = jnp.where(keep, jnp.concatenate(scores, axis=0), 0.0).astype(BF16)
    o_state = jnp.concatenate(o_state, axis=0)
    xt = jnp.concatenate([k_up, jnp.broadcast_to(last, (c_len, GLA_QK_W))], axis=0).T
    k_up_t = xt[:, 0:c_len].astype(BF16)
    decay_col = jnp.exp2(xt[:, c_len:c_len + 1])
    upd = []
    for h in range(GLA_HEADS):
        rows = slice(c_len * h, c_len * (h + 1))
        vh = vc[:, GLA_DV * h:GLA_DV * (h + 1)]
        o_h = _dot(p[rows], vh) + o_state[rows]
        dst = (0, pl.ds(pl.multiple_of(out_row0 + r0, c_len), c_len), slice(GLA_DV * h, GLA_DV * (h + 1)))
        out_ref[dst] = out_ref[dst] + o_h if accumulate else o_h
        upd.append(_dot(k_up_t[rows], vh))
    return decay_col * state + jnp.concatenate(upd, axis=0)


def _kernel_gla(gqkf_ref, gvf_ref, zf_ref, gqkb_ref, gvb_ref, zb_ref,
                w2f_ref, b2f_ref, w2b_ref, b2b_ref, lower_ref, upper_ref,
                o_ref, state_ref):
    j = pl.program_id(1)
    nb = pl.num_programs(1)

    @pl.when(j == 0)
    def _():
        state_ref[...] = jnp.zeros_like(state_ref)

    tn = gqkf_ref.shape[1]
    pl.when(j < nb // 2)(functools.partial(
        _gla_step, gqkf_ref, gvf_ref, zf_ref, gqkb_ref, gvb_ref, zb_ref, w2f_ref, b2f_ref, w2b_ref, b2b_ref,
        lower_ref, upper_ref, o_ref, state_ref, j * tn, (nb - 1 - j) * tn, False))
    pl.when(j >= nb // 2)(functools.partial(
        _gla_step, gqkf_ref, gvf_ref, zf_ref, gqkb_ref, gvb_ref, zb_ref, w2f_ref, b2f_ref, w2b_ref, b2b_ref,
        lower_ref, upper_ref, o_ref, state_ref, j * tn, (nb - 1 - j) * tn, True))


def _gla_step(gqkf_ref, gvf_ref, zf_ref, gqkb_ref, gvb_ref, zb_ref, w2f_ref, b2f_ref, w2b_ref, b2b_ref,
              lower_ref, upper_ref, o_ref, state_ref, row_f, row_b, accumulate):
    c_len = GLA_CHUNK
    n_chunks = gqkf_ref.shape[1] // c_len
    cum_f = _gla_log_decay_cumsum(zf_ref, w2f_ref, b2f_ref, lower_ref)
    cum_b = _gla_log_decay_cumsum(zb_ref, w2b_ref, b2b_ref, upper_ref)
    first_half = lax.broadcasted_iota(jnp.int32, (c_len, LANES), 1) < GLA_DK
    half_mask = [first_half.astype(BF16), (~first_half).astype(BF16)]
    row = lax.broadcasted_iota(jnp.int32, (GLA_HEADS * c_len, c_len), 0) % c_len
    col = lax.broadcasted_iota(jnp.int32, (GLA_HEADS * c_len, c_len), 1)
    state_f, state_b = state_ref[0], state_ref[1]
    for c in range(n_chunks):
        state_f = _gla_chunk(gqkf_ref, gvf_ref, o_ref, row_f, accumulate, cum_f, state_f, c, half_mask,
                             col <= row, False)
        state_b = _gla_chunk(gqkb_ref, gvb_ref, o_ref, row_b, accumulate, cum_b, state_b, n_chunks - 1 - c,
                             half_mask, col > row, True)
    state_ref[0] = state_f
    state_ref[1] = state_b


def _call_gla(gqk, gv, z, w):
    bsz, n, _ = gqk.shape
    tn = GLA_TILE
    nb = n // tn
    fwd = lambda width: pl.BlockSpec((1, tn, width), lambda b, j: (b, j, 0))
    bwd = lambda width: pl.BlockSpec((1, tn, width), lambda b, j: (b, nb - 1 - j, 0))
    assert nb % 2 == 0
    return pl.pallas_call(
        _kernel_gla,
        grid=(bsz, nb),
        in_specs=[fwd(2 * GLA_QK_W), fwd(GLA_V_W), fwd(Z_W), bwd(2 * GLA_QK_W), bwd(GLA_V_W), bwd(Z_W),
                  _const_spec((Z_W, GLA_QK_W)), _const_spec((1, GLA_QK_W)),
                  _const_spec((Z_W, GLA_QK_W)), _const_spec((1, GLA_QK_W)),
                  _const_spec((CUM_TILE, CUM_TILE)), _const_spec((CUM_TILE, CUM_TILE))],
        out_specs=pl.BlockSpec((1, n, GLA_V_W), lambda b, j: (b, 0, 0)),
        out_shape=jax.ShapeDtypeStruct((bsz, n, GLA_V_W), F32),
        scratch_shapes=[pltpu.VMEM((2, GLA_HEADS * GLA_DK, GLA_DV), F32)],
        compiler_params=pltpu.CompilerParams(dimension_semantics=("arbitrary", "arbitrary"),
                                             vmem_limit_bytes=VMEM_LIMIT),
        name="gla_bidir",
    )(gqk, gv, z, gqk, gv, z, w["w2f"], w["b2f"], w["w2b"], w["b2b"], w["tri_lower"], w["tri_upper"])


def _kernel_attn(flag_ref, qt_ref, k_ref, vt_ref, o_ref, st_ref, m_ref):
    n = k_ref.shape[2]
    sub = ATTN_UNIT_ROWS
    hpu = ATTN_UNIT_HEADS
    width = hpu * sub
    groups = ATTN_HEADS // ATTN_KV_HEADS // hpu
    n_units = (ATTN_Q_TILE // sub) * groups
    tiles = [slice(t, t + ATTN_KEY_TILE) for t in range(0, n, ATTN_KEY_TILE)]
    step = pl.program_id(2)
    last_step = pl.num_programs(2) - 1

    def unit_window(at_step, u):
        s, hg = divmod(u, groups)
        return at_step * (ATTN_Q_TILE // sub) + s, hg

    def run_phase(u_scores, scores_step, u_pv):
        if u_scores is not None:
            tile_idx, hg = unit_window(scores_step, u_scores)
            heads = [qt_ref[0, tile_idx, ATTN_DH * h:ATTN_DH * (h + 1), :]
                     for h in range(hpu * hg, hpu * (hg + 1))]
            q_cur = jnp.concatenate(heads, axis=1)
            m_run = None
        if u_pv is not None:
            m_fin = m_ref[u_pv]
            acc = jnp.zeros((VT_ROWS, width), F32)
        for idx in range(len(tiles)):
            if u_pv is not None:
                tile = tiles[idx]
                pt = jnp.exp2(st_ref[u_pv % 2, tile, :] - m_fin).astype(BF16)
                acc = acc + _dot(vt_ref[0, 0, :, tile], pt)
            if u_scores is not None:
                tile = tiles[(idx + len(tiles) // 2) % len(tiles)]
                st = _dot(k_ref[0, 0, tile, :], q_cur)
                st_ref[u_scores % 2, tile, :] = st
                m_tile = jnp.max(st, axis=0, keepdims=True)
                m_run = m_tile if m_run is None else jnp.maximum(m_run, m_tile)
        if u_scores is not None:
            m_ref[u_scores] = m_run
        if u_pv is not None:
            tile_idx, hg = unit_window(step, u_pv)
            row0 = pl.multiple_of(tile_idx * sub, sub)
            o2 = (acc[0:ATTN_DH] / acc[ATTN_DH:ATTN_DH + 1]).T
            o_ref[0, pl.ds(row0, sub), ATTN_DH * hpu * hg:ATTN_DH * hpu * (hg + 1)] = (
                jnp.concatenate([o2[sub * i:sub * (i + 1)] for i in range(hpu)], axis=1).astype(BF16))

    pl.when(step == 0)(functools.partial(run_phase, 0, step, None))
    for u in range(n_units - 1):
        pl.when(flag_ref[u] == 0)(functools.partial(run_phase, u + 1, step, u))
    pl.when(step < last_step)(functools.partial(run_phase, 0, step + 1, n_units - 1))
    pl.when(step == last_step)(functools.partial(run_phase, None, None, n_units - 1))


def _call_attn(qt, k, vt):
    bsz, n = qt.shape[0], k.shape[2]
    group_w = ATTN_Q_W // ATTN_KV_HEADS
    width = ATTN_UNIT_HEADS * ATTN_UNIT_ROWS
    n_units = (ATTN_Q_TILE // ATTN_UNIT_ROWS) * (group_w // (ATTN_UNIT_HEADS * ATTN_DH))
    assert n_units % 2 == 0
    qt_spec = pl.BlockSpec((1, n // ATTN_UNIT_ROWS, group_w, ATTN_UNIT_ROWS), lambda b, g, i: (b, 0, g, 0))
    o_spec = pl.BlockSpec((1, n, group_w), lambda b, g, i: (b, 0, g))
    k_spec = pl.BlockSpec((1, 1, n, ATTN_DH), lambda b, g, i: (b, g, 0, 0))
    vt_spec = pl.BlockSpec((1, 1, VT_ROWS, n), lambda b, g, i: (b, g, 0, 0))
    return pl.pallas_call(
        _kernel_attn,
        grid=(bsz, ATTN_KV_HEADS, n // ATTN_Q_TILE),
        in_specs=[pl.BlockSpec(memory_space=pltpu.SMEM), qt_spec, k_spec, vt_spec],
        out_specs=o_spec,
        out_shape=jax.ShapeDtypeStruct((bsz, n, ATTN_Q_W), BF16),
        scratch_shapes=[pltpu.VMEM((2, n, width), F32),
                        pltpu.VMEM((n_units, 1, width), F32)],
        compiler_params=pltpu.CompilerParams(
            dimension_semantics=("arbitrary", "arbitrary", "arbitrary"),
            vmem_limit_bytes=VMEM_LIMIT),
        name="gqa_attention",
    )(jnp.zeros((n_units,), jnp.int32), qt, k, vt)


def _kernel_b(h1_ref, og_ref, gr_ref, oa_ref, p_ref, gn_ref, wog_ref, woa_ref,
              ln2g_ref, ln2b_ref, wg_ref, wu_ref, wd_ref, ln3g_ref, ln3b_ref, wpg_ref, bpg_ref, wpe_ref,
              out_ref):
    o = og_ref[0]
    gn = gn_ref[...]
    normed = []
    for h in range(GLA_HEADS):
        oh = o[:, GLA_DV * h:GLA_DV * (h + 1)]
        ms = jnp.mean(oh * oh, axis=-1, keepdims=True)
        normed.append(oh * lax.rsqrt(ms + GN_EPS) * gn)
    gr = gr_ref[0]
    o_gla = jnp.concatenate(normed, axis=1) * (gr * jax.nn.sigmoid(gr))
    mix = _dot(o_gla.astype(BF16), wog_ref[...]) + _dot(oa_ref[0], woa_ref[...])
    h2 = _layer_norm(DEEPNORM_ALPHA * h1_ref[0] + mix, ln2g_ref[...], ln2b_ref[...])
    f = _swiglu(h2.astype(BF16), wg_ref, wu_ref, wd_ref)
    h3 = _layer_norm(DEEPNORM_ALPHA * h2 + 0.5 * f, ln3g_ref[...], ln3b_ref[...])
    gate = jax.nn.sigmoid(_dot(h3.astype(BF16), wpg_ref[...]) + bpg_ref[...])
    out_ref[0] = h3 + gate * _dot(p_ref[0].astype(BF16), wpe_ref[...])


def _call_b(h1, o_gla, gr, o_att, p, w):
    bsz, n, _ = h1.shape
    tm = TOKEN_TILE
    tok = lambda width: pl.BlockSpec((1, tm, width), lambda b, i: (b, i, 0))
    return pl.pallas_call(
        _kernel_b,
        grid=(bsz, n // tm),
        in_specs=[tok(D_MODEL), tok(GLA_V_W), tok(GLA_V_W), tok(ATTN_Q_W), tok(P_DIM),
                  _const_spec((1, GLA_DV)),
                  _const_spec((GLA_V_W, D_MODEL)), _const_spec((ATTN_Q_W, D_MODEL)),
                  _const_spec((1, D_MODEL)), _const_spec((1, D_MODEL)),
                  _const_spec((D_MODEL, D_FF)), _const_spec((D_MODEL, D_FF)), _const_spec((D_FF, D_MODEL)),
                  _const_spec((1, D_MODEL)), _const_spec((1, D_MODEL)),
                  _const_spec((D_MODEL, D_MODEL)), _const_spec((1, D_MODEL)), _const_spec((P_DIM, D_MODEL))],
        out_specs=tok(D_MODEL),
        out_shape=jax.ShapeDtypeStruct((bsz, n, D_MODEL), F32),
        compiler_params=pltpu.CompilerParams(dimension_semantics=("arbitrary", "arbitrary"),
                                             vmem_limit_bytes=VMEM_LIMIT),
        name="outproj_ffn2_embed",
    )(h1, o_gla, gr, o_att, p, w["gn_g"], w["w_out_gla"], w["w_out_att"], w["ln2_g"], w["ln2_b"],
      w["ffn2_wg"], w["ffn2_wu"], w["ffn2_wd"], w["ln3_g"], w["ln3_b"], w["w_pg"], w["b_pg"], w["w_pe"])


def _rope_tables(n):
    t = jnp.arange(n, dtype=jnp.int32)
    row = (t // GRID_W).astype(F32)
    col = (t % GRID_W).astype(F32)
    axis_dim = ATTN_DH // 2
    inv_freq = ROPE_THETA ** (-jnp.arange(0, axis_dim, 2, dtype=F32) / axis_dim)
    lane = jnp.arange(LANES, dtype=jnp.int32) % ATTN_DH
    freq = inv_freq[lane % ROPE_HALF]
    pos = jnp.where((lane // axis_dim)[None, :] == 0, row[:, None], col[:, None])
    ang = pos * freq[None, :]
    first_half = ((lane % axis_dim) // ROPE_HALF == 0)[None, :]
    cos, sin = jnp.cos(ang), jnp.sin(ang)
    return cos, jnp.where(first_half, -sin, 0.0), jnp.where(first_half, 0.0, sin)


def _prepare(ffn1_wg, ffn1_wu, ffn1_wd, ln1_g, ln1_b, w_in, gla_w2f, gla_b2f, gla_w2b, gla_b2b,
             gla_gn_g, q_norm_g, k_norm_g, w_out, ln2_g, ln2_b, ffn2_wg, ffn2_wu, ffn2_wd,
             ln3_g, ln3_b, w_pg, b_pg, w_pe):
    row = lambda v: v.reshape(1, -1).astype(F32)
    z0 = 2 * GLA_QK_W + 2 * GLA_V_W
    w_in_r = jnp.concatenate([w_in[:, :z0], w_in[:, z0 + Z_W:], w_in[:, z0:z0 + Z_W]], axis=1)
    zeros = jnp.zeros((GLA_GATE_RANK, GLA_QK_W), F32)
    idx = jnp.arange(AVG_W)
    avg =jnp.where((idx[:, None] // ATTN_DH) == (idx[None, :] // ATTN_DH), 1.0 / ATTN_DH, 0.0)
    ci = jnp.arange(CUM_TILE)
    same_chunk = (ci[:, None] // GLA_CHUNK) == (ci[None, :] // GLA_CHUNK)
    lower = jnp.where(same_chunk & (ci[None, :] <= ci[:, None]), 1.0, 0.0)
    return {
        "ffn1_wg": ffn1_wg.astype(BF16), "ffn1_wu": ffn1_wu.astype(BF16), "ffn1_wd": ffn1_wd.astype(BF16),
        "ln1_g": row(ln1_g), "ln1_b": row(ln1_b),
        "w_in": w_in_r.astype(BF16), "avg": avg.astype(BF16),
        "q_gain": row(jnp.tile(q_norm_g, ATTN_HEADS)), "k_gain": row(jnp.tile(k_norm_g, ATTN_KV_HEADS)),
        "w2f": jnp.concatenate([gla_w2f, zeros], axis=0).astype(BF16), "b2f": row(gla_b2f),
        "w2b": jnp.concatenate([zeros, gla_w2b], axis=0).astype(BF16), "b2b": row(gla_b2b),
        "tri_lower": lower.astype(BF16), "tri_upper": lower.T.astype(BF16),
        "gn_g": row(gla_gn_g),
        "w_out_gla": w_out[:GLA_V_W].astype(BF16), "w_out_att": w_out[GLA_V_W:].astype(BF16),
        "ln2_g": row(ln2_g), "ln2_b": row(ln2_b),
        "ffn2_wg": ffn2_wg.astype(BF16), "ffn2_wu": ffn2_wu.astype(BF16), "ffn2_wd": ffn2_wd.astype(BF16),
        "ln3_g": row(ln3_g), "ln3_b": row(ln3_b),
        "w_pg": w_pg.astype(BF16), "b_pg": row(b_pg), "w_pe": w_pe.astype(BF16),
    }


def _encoder_layer(x, p, w):
    n = x.shape[1]
    assert n % GLA_TILE == 0 and n % ATTN_Q_TILE == 0 and n % TOKEN_TILE == 0 and n % GRID_W == 0
    h1, gqk, gv, gr, z, q, k, v = _call_a(x, w, _rope_tables(n))
    o_gla = _call_gla(gqk, gv, z, w)
    o_att = _call_attn(q, k, v)
    return _call_b(h1, o_gla, gr, o_att, p, w)


def kernel(x_prompt, x_sample, p_prompt, p_sample, ffn1_wg, ffn1_wu, ffn1_wd, ln1_g, ln1_b, w_in,
           gla_w2f, gla_b2f, gla_w2b, gla_b2b, gla_gn_g, q_norm_g, k_norm_g, w_out, ln2_g, ln2_b,
           ffn2_wg, ffn2_wu, ffn2_wd, ln3_g, ln3_b, w_pg, b_pg, w_pe):
    y_prompt, y_sample = x_prompt, x_sample
    for i in range(DEPTH):
        w = _prepare(ffn1_wg[i], ffn1_wu[i], ffn1_wd[i], ln1_g[i], ln1_b[i], w_in[i],
                     gla_w2f[i], gla_b2f[i], gla_w2b[i], gla_b2b[i], gla_gn_g[i], q_norm_g[i],
                     k_norm_g[i], w_out[i], ln2_g[i], ln2_b[i], ffn2_wg[i], ffn2_wu[i], ffn2_wd[i],
                     ln3_g[i], ln3_b[i], w_pg[i], b_pg[i], w_pe[i])
        y_prompt = _encoder_layer(y_prompt, p_prompt[i], w)
        y_sample = _encoder_layer(y_sample, p_sample[i], w)
    return (y_prompt, y_sample)
```
